```python
import jax, jax.numpy as jnp
from jax import lax
import numpy as np

D_MODEL = 1024
BATCH = 8
SEQ = 2048
DEPTH = 2
DEC_BATCH = 32
DEC_SEQ = 1
PAST_LEN = 16384
PAGE_SIZE = 128

A_HEADS = 4
A_KV_HEADS = 2
A_HEAD_DIM = 128
A_WIDTH = A_HEADS * A_HEAD_DIM
IDX_HEADS = 8
IDX_DIM = 64
TOPK_MAX = 256
CHUNK = 128
B_GROUPS = 4
B_GROUP_DIM = 128
B_WIDTH = B_GROUPS * B_GROUP_DIM
C_HEADS = 4
Q_LORA = 256
KV_LORA = 256
C_NOPE = 128
C_ROPE = 64
C_VDIM = 128
C_WIDTH = C_HEADS * C_VDIM
N_BRANCH = 3
D_FF = 2816
CONV_W = 3
ROPE_THETA = 10000.0
EPS = 1e-6
Q_BLOCK = 128
IN_SPLITS = (A_WIDTH, A_KV_HEADS * A_HEAD_DIM, A_KV_HEADS * A_HEAD_DIM, IDX_HEADS * IDX_DIM, IDX_DIM, IDX_HEADS, B_WIDTH, B_WIDTH, Q_LORA, KV_LORA, C_ROPE, N_BRANCH * D_MODEL)
D_IN = A_WIDTH + 2 * A_KV_HEADS * A_HEAD_DIM + IDX_HEADS * IDX_DIM + IDX_DIM + IDX_HEADS + 2 * B_WIDTH + Q_LORA + KV_LORA + C_ROPE + N_BRANCH * D_MODEL

kernel_name = 'hybrid_dsa_gmlp_mla_convffn_step'


def rms_norm(x, g):
    xf = x.astype(jnp.float32)
    y = xf * lax.rsqrt(jnp.mean(xf * xf, axis=-1, keepdims=True) + EPS)
    return (y * g.astype(jnp.float32)).astype(x.dtype)


def rope(x, pos):
    half = x.shape[-1] // 2
    inv = ROPE_THETA ** (-jnp.arange(half, dtype=jnp.float32) / half)
    ang = pos.astype(jnp.float32)[:, None] * inv[None, :]
    ang = ang.reshape((pos.shape[0],) + (1,) * (x.ndim - 3) + (half,))
    cos, sin = jnp.cos(ang), jnp.sin(ang)
    xf = x.astype(jnp.float32)
    x1, x2 = xf[..., :half], xf[..., half:]
    return jnp.concatenate([x1 * cos - x2 * sin, x2 * cos + x1 * sin], axis=-1).astype(x.dtype)


def split_cols(z):
    offs, acc = [], 0
    for s in IN_SPLITS[:-1]:
        acc += s
        offs.append(acc)
    return jnp.split(z, offs, axis=-1)


def project(h, pos, p):
    bsz, L, _ = h.shape
    qa, ka, va, qi, ki, wi, u, v, cq, ckv, kr, gl = split_cols(h @ p['w_in'])
    qc = jnp.einsum('bla,ahd->blhd', rms_norm(cq, p['c_q_norm']), p['c_w_uq'])
    q_nope, q_rope = qc[..., :C_NOPE], qc[..., C_NOPE:]
    return {
        'qa': rope(qa.reshape(bsz, L, A_HEADS, A_HEAD_DIM), pos),
        'ka': rope(ka.reshape(bsz, L, A_KV_HEADS, A_HEAD_DIM), pos),
        'va': va.reshape(bsz, L, A_KV_HEADS, A_HEAD_DIM),
        'qi': rope(qi.reshape(bsz, L, IDX_HEADS, IDX_DIM), pos),
        'ki': rope(ki, pos),
        'wi': wi * IDX_HEADS ** -0.5,
        'u': jax.nn.gelu(u),
        'v': rms_norm(jax.nn.gelu(v), p['b_v_norm']),
        'q_lat': jnp.einsum('blhn,chn->blhc', q_nope, p['c_w_uk']),
        'q_rope': rope(q_rope, pos),
        'c_kv': rms_norm(ckv, p['c_kv_norm']),
        'k_rope': rope(kr, pos),
        'gates': jax.nn.sigmoid(gl.reshape(bsz, L, N_BRANCH, D_MODEL)),
    }


def blockwise(fn, q_pos, *qs):
    L = q_pos.shape[0]
    qb = min(Q_BLOCK, L)
    nb = -(-L // qb)
    pad = nb * qb - L

    def prep(a):
        a = jnp.pad(a, [(0, 0), (0, pad)] + [(0, 0)] * (a.ndim - 2))
        return jnp.moveaxis(a.reshape((a.shape[0], nb, qb) + a.shape[2:]), 1, 0)

    pos_b = jnp.pad(q_pos, (0, pad), mode='edge').reshape(nb, qb)
    out = lax.map(lambda args: fn(*args), (pos_b,) + tuple(prep(a) for a in qs))
    out = jnp.moveaxis(out, 0, 1)
    out = out.reshape((out.shape[0], nb * qb) + out.shape[3:])
    return out[:, :L]


def dsa_block(q_pos, qa, qi, wi, ki_all, k_pos, fetch, topk):
    s_idx = jnp.einsum('bqhd,bkd->bqhk', qi, ki_all) * IDX_DIM ** -0.5
    score = jnp.einsum('bqh,bqhk->bqk', wi, jax.nn.relu(s_idx)).astype(jnp.float32)
    score = jnp.where(k_pos[None, None, :] <= q_pos[None, :, None], score, -jnp.inf)
    top_val, sel = lax.top_k(score, topk)
    valid = jnp.isfinite(top_val)
    k_sel, v_sel = fetch(sel)
    bsz, qb = qa.shape[:2]
    qg = qa.reshape(bsz, qb, A_KV_HEADS, A_HEADS // A_KV_HEADS, A_HEAD_DIM)
    s = jnp.einsum('bqgrd,bqkgd->bqgrk', qg, k_sel).astype(jnp.float32) * A_HEAD_DIM ** -0.5
    s = jnp.where(valid[:, :, None, None, :], s, -jnp.inf)
    prob = jax.nn.softmax(s, axis=-1).astype(v_sel.dtype)
    o = jnp.einsum('bqgrk,bqkgd->bqgrd', prob, v_sel)
    return o.reshape(bsz, qb, A_WIDTH)


def mla_block(q_pos, q_lat, q_rope, c_kv_all, k_rope_all, k_pos):
    s = (jnp.einsum('bqhc,bkc->bhqk', q_lat, c_kv_all) + jnp.einsum('bqhr,bkr->bhqk', q_rope, k_rope_all)).astype(jnp.float32) * (C_NOPE + C_ROPE) ** -0.5
    s = jnp.where(k_pos[None, None, None, :] <= q_pos[None, None, :, None], s, -jnp.inf)
    prob = jax.nn.softmax(s, axis=-1).astype(c_kv_all.dtype)
    return jnp.einsum('bhqk,bkc->bqhc', prob, c_kv_all)


def chunk_spatial_gate(u, v, w_s, bias):
    bsz, L, _ = v.shape
    nc = -(-L // CHUNK)
    vp = jnp.pad(v, [(0, 0), (0, nc * CHUNK - L), (0, 0)]).reshape(bsz, nc, CHUNK, B_GROUPS, B_GROUP_DIM)
    w = w_s * jnp.tril(jnp.ones((CHUNK, CHUNK), w_s.dtype))
    mixed = jnp.einsum('gts,bcsgd->bctgd', w, vp) + jnp.swapaxes(bias, 0, 1)[None, None, :, :, None]
    return u * mixed.reshape(bsz, nc * CHUNK, B_WIDTH)[:, :L]


def mixer_out(pr, oa, o_lat, p):
    bsz, L = oa.shape[:2]
    ob = chunk_spatial_gate(pr['u'], pr['v'], p['b_w_s'], p['b_bias'])
    oc = jnp.einsum('blhc,chv->blhv', o_lat, p['c_w_uv']).reshape(bsz, L, C_WIDTH)
    g = pr['gates']
    merged = g[:, :, 0] * (oa @ p['w_br_a']) + g[:, :, 1] * (ob @ p['w_br_b']) + g[:, :, 2] * (oc @ p['w_br_c'])
    return merged @ p['w_out']


def conv_ffn(h, hist, p):
    L = h.shape[1]
    a, g = jnp.split(h @ p['ffn_w_up'], 2, axis=-1)
    g_ext = jnp.concatenate([hist.astype(g.dtype), g], axis=1)
    w = p['ffn_conv_w']
    conv = g_ext[:, CONV_W - 1:CONV_W - 1 + L] * w[CONV_W - 1] + p['ffn_conv_b']
    for j in range(CONV_W - 1):
        conv = conv + g_ext[:, j:j + L] * w[j]
    y = (jax.nn.silu(conv) * a) @ p['ffn_w_down']
    return y, g_ext[:, g_ext.shape[1] - (CONV_W - 1):]


def setup_inputs(seed: int = 0) -> dict:
    key = jax.random.key(seed)
    ks = iter(jax.random.split(key, 40))
    f32 = jnp.float32

    def nrm(shape, scale=1.0):
        return jax.random.normal(next(ks), shape, f32) * scale

    def gain(shape):
        return 1.0 + nrm(shape, 0.02)

    n_pages = PAST_LEN // PAGE_SIZE
    n_used = DEC_BATCH * n_pages
    n_phys = n_used + (n_used + 3) // 4
    page_table = jax.random.permutation(next(ks), n_phys)[:n_used].reshape(DEC_BATCH, n_pages).astype(jnp.int32)
    row_scale = (jnp.arange(1, CHUNK + 1, dtype=f32) ** -0.5)[:, None]
    x_prompt = nrm((BATCH, SEQ, D_MODEL))
    x_sample = nrm((DEC_BATCH, DEC_SEQ, D_MODEL))
    cache_a_k = nrm((DEPTH, n_phys, PAGE_SIZE, A_KV_HEADS, A_HEAD_DIM))
    cache_a_v = nrm((DEPTH, n_phys, PAGE_SIZE, A_KV_HEADS, A_HEAD_DIM))
    cache_a_idxk = nrm((DEPTH, n_phys, PAGE_SIZE, IDX_DIM))
    cache_c_latent = nrm((DEPTH, n_phys, PAGE_SIZE, KV_LORA))
    cache_c_krope = nrm((DEPTH, n_phys, PAGE_SIZE, C_ROPE))
    state_ffn_conv = nrm((DEPTH, DEC_BATCH, CONV_W - 1, D_FF))
    return {
        'x_prompt': x_prompt,
        'x_sample': x_sample,
        'cache_a_k': cache_a_k,
        'cache_a_v': cache_a_v,
        'cache_a_idxk': cache_a_idxk,
        'cache_c_latent': cache_c_latent,
        'cache_c_krope': cache_c_krope,
        'state_ffn_conv': state_ffn_conv,
        'page_table': page_table,
        'attn_norm': gain((DEPTH, D_MODEL)),
        'w_in': nrm((DEPTH, D_MODEL, D_IN), D_MODEL ** -0.5),
        'c_q_norm': gain((DEPTH, Q_LORA)),
        'c_w_uq': nrm((DEPTH, Q_LORA, C_HEADS, C_NOPE + C_ROPE), Q_LORA ** -0.5),
        'c_kv_norm': gain((DEPTH, KV_LORA)),
        'c_w_uk': nrm((DEPTH, KV_LORA, C_HEADS, C_NOPE), KV_LORA ** -0.5),
        'c_w_uv': nrm((DEPTH, KV_LORA, C_HEADS, C_VDIM), KV_LORA ** -0.5),
        'b_v_norm': gain((DEPTH, B_WIDTH)),
        'b_w_s': nrm((DEPTH, B_GROUPS, CHUNK, CHUNK)) * row_scale,
        'b_bias': nrm((DEPTH, B_GROUPS, CHUNK), 0.02),
        'w_br_a': nrm((DEPTH, A_WIDTH, D_MODEL), A_WIDTH ** -0.5),
        'w_br_b': nrm((DEPTH, B_WIDTH, D_MODEL), B_WIDTH ** -0.5),
        'w_br_c': nrm((DEPTH, C_WIDTH, D_MODEL), C_WIDTH ** -0.5),
        'w_out': nrm((DEPTH, D_MODEL, D_MODEL), D_MODEL ** -0.5),
        'ffn_norm': gain((DEPTH, D_MODEL)),
        'ffn_w_up': nrm((DEPTH, D_MODEL, 2 * D_FF), D_MODEL ** -0.5),
        'ffn_conv_w': nrm((DEPTH, CONV_W, D_FF), CONV_W ** -0.5),
        'ffn_conv_b': nrm((DEPTH, D_FF), 0.02),
        'ffn_w_down': nrm((DEPTH, D_FF, D_MODEL), D_FF ** -0.5),
        'final_norm': gain((D_MODEL,)),
    }


def reference(x_prompt, x_sample, cache_a_k, cache_a_v, cache_a_idxk, cache_c_latent, cache_c_krope, state_ffn_conv, page_table, attn_norm, w_in, c_q_norm, c_w_uq, c_kv_norm, c_w_uk, c_w_uv, b_v_norm, b_w_s, b_bias, w_br_a, w_br_b, w_br_c, w_out, ffn_norm, ffn_w_up, ffn_conv_w, ffn_conv_b, ffn_w_down, final_norm):
    bsz, seq, _ = x_prompt.shape
    dbsz, dseq, _ = x_sample.shape
    past_len = page_table.shape[1] * PAGE_SIZE
    pos_p = jnp.arange(seq, dtype=jnp.int32)
    pos_s = past_len + jnp.arange(dseq, dtype=jnp.int32)
    kpos_s = jnp.arange(past_len + dseq, dtype=jnp.int32)
    topk_p = min(TOPK_MAX, seq // 4)
    topk_s = min(TOPK_MAX, (past_len + dseq) // 4)
    bidx_p = jnp.arange(bsz)[:, None, None]
    bidx_s = jnp.arange(dbsz)[:, None, None]
    xp, xs = x_prompt, x_sample
    pk, pv, pik, pcl, pcr, pfc = [], [], [], [], [], []
    sk, sv, sik, scl, scr, sfc, sbv = [], [], [], [], [], [], []
    for l in range(DEPTH):
        p = {'w_in': w_in[l], 'c_q_norm': c_q_norm[l], 'c_w_uq': c_w_uq[l], 'c_kv_norm': c_kv_norm[l], 'c_w_uk': c_w_uk[l], 'c_w_uv': c_w_uv[l], 'b_v_norm': b_v_norm[l], 'b_w_s': b_w_s[l], 'b_bias': b_bias[l], 'w_br_a': w_br_a[l], 'w_br_b': w_br_b[l], 'w_br_c': w_br_c[l], 'w_out': w_out[l], 'ffn_w_up': ffn_w_up[l], 'ffn_conv_w': ffn_conv_w[l], 'ffn_conv_b': ffn_conv_b[l], 'ffn_w_down': ffn_w_down[l]}

        pr = project(rms_norm(xp, attn_norm[l]), pos_p, p)

        def fetch_p(sel, pr=pr):
            return pr['ka'][bidx_p, sel], pr['va'][bidx_p, sel]

        oa = blockwise(lambda qpos, qa, qi, wi: dsa_block(qpos, qa, qi, wi, pr['ki'], pos_p, fetch_p, topk_p), pos_p, pr['qa'], pr['qi'], pr['wi'])
        ol = blockwise(lambda qpos, ql, qr: mla_block(qpos, ql, qr, pr['c_kv'], pr['k_rope'], pos_p), pos_p, pr['q_lat'], pr['q_rope'])
        xp = xp + mixer_out(pr, oa, ol, p)
        y, hist_p = conv_ffn(rms_norm(xp, ffn_norm[l]), jnp.zeros((bsz, CONV_W - 1, D_FF), xp.dtype), p)
        xp = xp + y
        pk.append(pr['ka']); pv.append(pr['va']); pik.append(pr['ki'])
        pcl.append(pr['c_kv']); pcr.append(pr['k_rope']); pfc.append(hist_p)

        sr = project(rms_norm(xs, attn_norm[l]), pos_s, p)
        ki_all = jnp.concatenate([cache_a_idxk[l][page_table].reshape(dbsz, past_len, IDX_DIM), sr['ki']], axis=1)
        k_pool, v_pool = cache_a_k[l], cache_a_v[l]

        def fetch_s(sel, sr=sr, k_pool=k_pool, v_pool=v_pool):
            in_past = (sel < past_len)[..., None, None]
            ps = jnp.minimum(sel, past_len - 1)
            phys = page_table[bidx_s, ps // PAGE_SIZE]
            off = ps % PAGE_SIZE
            ns = jnp.clip(sel - past_len, 0, dseq - 1)
            k_sel = jnp.where(in_past, k_pool[phys, off], sr['ka'][bidx_s, ns])
            v_sel = jnp.where(in_past, v_pool[phys, off], sr['va'][bidx_s, ns])
            return k_sel, v_sel

        oa_s = blockwise(lambda qpos, qa, qi, wi: dsa_block(qpos, qa, qi, wi, ki_all, kpos_s, fetch_s, topk_s), pos_s, sr['qa'], sr['qi'], sr['wi'])
        c_all = jnp.concatenate([cache_c_latent[l][page_table].reshape(dbsz, past_len, KV_LORA), sr['c_kv']], axis=1)
        kr_all = jnp.concatenate([cache_c_krope[l][page_table].reshape(dbsz, past_len, C_ROPE), sr['k_rope']], axis=1)
        ol_s = blockwise(lambda qpos, ql, qr: mla_block(qpos, ql, qr, c_all, kr_all, kpos_s), pos_s, sr['q_lat'], sr['q_rope'])
        xs = xs + mixer_out(sr, oa_s, ol_s, p)
        y_s, hist_s = conv_ffn(rms_norm(xs, ffn_norm[l]), state_ffn_conv[l], p)
        xs = xs + y_s
        sk.append(sr['ka']); sv.append(sr['va']); sik.append(sr['ki'])
        scl.append(sr['c_kv']); scr.append(sr['k_rope']); sfc.append(hist_s); sbv.append(sr['v'])

    y_prompt = rms_norm(xp, final_norm)
    y_sample = rms_norm(xs, final_norm)
    p_a_k, p_a_v, p_a_idxk = jnp.stack(pk), jnp.stack(pv), jnp.stack(pik)
    p_c_latent, p_c_krope, p_ffn_conv = jnp.stack(pcl), jnp.stack(pcr), jnp.stack(pfc)
    s_a_k, s_a_v, s_a_idxk = jnp.stack(sk), jnp.stack(sv), jnp.stack(sik)
    s_c_latent, s_c_krope, s_ffn_conv = jnp.stack(scl), jnp.stack(scr), jnp.stack(sfc)
    s_b_v = jnp.stack(sbv)
    return (y_prompt, y_sample, p_a_k, p_a_v, p_a_idxk, p_c_latent, p_c_krope, p_ffn_conv, s_a_k, s_a_v, s_a_idxk, s_c_latent, s_c_krope, s_ffn_conv, s_b_v)
```

```python
import functools

import jax
import jax.numpy as jnp
from jax import lax
from jax.experimental import pallas as pl
from jax.experimental.pallas import tpu as pltpu

D_MODEL = 1024
PAGE_SIZE = 128
A_HEADS = 4
A_KV_HEADS = 2
A_HEAD_DIM = 128
A_WIDTH = A_HEADS * A_HEAD_DIM
A_KV_WIDTH = A_KV_HEADS * A_HEAD_DIM
IDX_HEADS = 8
IDX_DIM = 64
IDX_WIDTH = IDX_HEADS * IDX_DIM
TOPK_MAX = 256
CHUNK = 128
B_GROUPS = 4
B_GROUP_DIM = 128
B_WIDTH = B_GROUPS * B_GROUP_DIM
C_HEADS = 4
Q_LORA = 256
KV_LORA = 256
C_NOPE = 128
C_ROPE = 64
C_VDIM = 128
C_WIDTH = C_HEADS * C_VDIM
N_BRANCH = 3
D_FF = 2816
CONV_W = 3
ROPE_THETA = 10000.0
EPS = 1e-6

LANES = 128
SUBLANES = 8
VMEM_LIMIT = 56 * 1024 * 1024

_O_QA = 0
_O_KA = _O_QA + A_WIDTH
_O_VA = _O_KA + A_KV_WIDTH
_O_QI = _O_VA + A_KV_WIDTH
_O_KIKR = _O_QI + IDX_WIDTH
_O_WI = _O_KIKR + LANES
_O_U = _O_WI + LANES
_O_V = _O_U + B_WIDTH
_O_CQ = _O_V + B_WIDTH
_O_CKV = _O_CQ + Q_LORA
_W_ALL = _O_CKV + KV_LORA

_NT = (((1,), (1,)), ((), ()))
_INT_MIN = -2147483648
_NEG_INF_KEY = -2139095041
_ROW_PAD = 16

f32 = jnp.float32
bf16 = jnp.bfloat16


def _rms(x, g):
    return x * lax.rsqrt(jnp.mean(x * x, axis=-1, keepdims=True) + EPS) * g


def _rope_full(x, cos, sin):
    parts = []
    for k in range(x.shape[1] // LANES):
        xs = x[:, k * LANES:(k + 1) * LANES]
        parts.append(xs * cos + pltpu.roll(xs, LANES // 2, 1) * sin)
    return parts[0] if len(parts) == 1 else jnp.concatenate(parts, axis=1)


def _rope_half(x, cos, sin_lo, sin_hi):
    parts = []
    for k in range(x.shape[1] // LANES):
        xs = x[:, k * LANES:(k + 1) * LANES]
        parts.append(xs * cos + pltpu.roll(xs, LANES - IDX_DIM // 2, 1) * sin_lo
                     + pltpu.roll(xs, IDX_DIM // 2, 1) * sin_hi)
    return parts[0] if len(parts) == 1 else jnp.concatenate(parts, axis=1)


def _order_key(score):
    bits = lax.bitcast_convert_type(score, jnp.int32)
    return bits ^ ((bits >> 31) & jnp.int32(0x7FFFFFFF))


def _count(mask):
    return jnp.sum(jnp.where(mask, 1.0, 0.0), axis=1, keepdims=True)


def _kth_largest_key(key, k):
    def body(i, tb):
        cand_b = tb | jnp.left_shift(jnp.int32(1), 31 - i)
        cnt = _count(key >= (cand_b ^ jnp.int32(_INT_MIN)))
        return jnp.where(cnt >= k, cand_b, tb)

    tb = lax.fori_loop(0, 32, body, jnp.zeros((key.shape[0], 1), jnp.int32))
    return tb ^ jnp.int32(_INT_MIN)


def _topk_membership(key, k, sel_ref):
    rows, n = key.shape
    thr = _kth_largest_key(key, k)
    gt = key > thr
    eq = key == thr
    need = k - _count(gt)
    sel_ref[...] = jnp.where(gt | eq, 1.0, 0.0)
    tie = (thr > _NEG_INF_KEY) & (_count(eq) > need)

    @pl.when(jnp.max(jnp.where(tie, 1.0, 0.0)) > 0.0)
    def _():
        r = lax.broadcasted_iota(jnp.int32, (LANES, LANES), 0)
        c = lax.broadcasted_iota(jnp.int32, (LANES, LANES), 1)
        upper = jnp.where(r < c, 1.0, 0.0).astype(bf16)
        run = jnp.zeros((rows, 1), f32)
        for j in range(n // LANES):
            sl = slice(j * LANES, (j + 1) * LANES)
            eq_j = jnp.where(eq[:, sl], 1.0, 0.0)
            before = jnp.dot(eq_j.astype(bf16), upper, preferred_element_type=f32) + run
            keep = gt[:, sl] | (eq[:, sl] & (before < need))
            sel_ref[:, sl] = jnp.where(keep, 1.0, 0.0)
            run = run + jnp.sum(eq_j, axis=1, keepdims=True)


def _const_spec(shape):
    nd = len(shape)
    return pl.BlockSpec(shape, lambda *_: (0,) * nd, pipeline_mode=pl.Buffered(1))


def _params(sem):
    return pltpu.CompilerParams(dimension_semantics=sem, vmem_limit_bytes=VMEM_LIMIT)


def _proj_kernel(x_ref, gain_ref, w_ref, cqn_ref, wuq_ref, wuk_ref, ckvn_ref, bvn_ref, ws_ref, bias_ref,
                 cosa_ref, sina_ref, cosi_ref, silo_ref, sihi_ref,
                 qa_ref, ka_ref, va_ref, qi_ref, ki_ref, kr_ref, wi_ref, ob_ref, ckv_ref, qlat_ref, qrope_ref,
                 *vn_ref, decode):
    tm = x_ref.shape[0]
    h = _rms(x_ref[...], gain_ref[...]).astype(bf16)
    z = jnp.dot(h, w_ref[...], preferred_element_type=f32)
    cosa, sina = cosa_ref[...], sina_ref[...]
    cosi, silo, sihi = cosi_ref[...], silo_ref[...], sihi_ref[...]

    qa_ref[...] = _rope_full(z[:, _O_QA:_O_KA], cosa, sina).astype(bf16)
    ka_ref[...] = _rope_full(z[:, _O_KA:_O_VA], cosa, sina)
    va_ref[...] = z[:, _O_VA:_O_QI]
    qi_ref[...] = _rope_half(z[:, _O_QI:_O_KIKR], cosi, silo, sihi).astype(bf16)
    kikr = _rope_half(z[:, _O_KIKR:_O_WI], cosi, silo, sihi)
    ki_ref[...] = kikr[:, :IDX_DIM]
    kr_ref[...] = kikr[:, IDX_DIM:]
    wi_ref[...] = z[:, _O_WI:_O_U] * IDX_HEADS ** -0.5

    u = jax.nn.gelu(z[:, _O_U:_O_V])
    v = _rms(jax.nn.gelu(z[:, _O_V:_O_CQ]), bvn_ref[...])
    if decode:
        vn_ref[0][...] = v
        ob_ref[...] = (u * (ws_ref[...] * v + bias_ref[...])).astype(bf16)
    else:
        r = lax.broadcasted_iota(jnp.int32, (CHUNK, CHUNK), 0)
        c = lax.broadcasted_iota(jnp.int32, (CHUNK, CHUNK), 1)
        vb = v.astype(bf16)
        bias = bias_ref[...]
        for g in range(B_GROUPS):
            w_g = jnp.where(r >= c, ws_ref[g], 0.0).astype(bf16)
            cols = slice(g * B_GROUP_DIM, (g + 1) * B_GROUP_DIM)
            for k in range(tm // CHUNK):
                rows = slice(k * CHUNK, (k + 1) * CHUNK)
                mixed = jnp.dot(w_g, vb[rows, cols], preferred_element_type=f32) + bias[:, g:g + 1]
                ob_ref[rows, cols] = (u[rows, cols] * mixed).astype(bf16)

    ckv_ref[...] = _rms(z[:, _O_CKV:_W_ALL], ckvn_ref[...])
    cq = _rms(z[:, _O_CQ:_O_CKV], cqn_ref[...]).astype(bf16)
    qc = jnp.dot(cq, wuq_ref[...], preferred_element_type=f32)
    for hd in range(C_HEADS):
        q_nope = qc[:, hd * C_NOPE:(hd + 1) * C_NOPE].astype(bf16)
        qlat_ref[:, hd * KV_LORA:(hd + 1) * KV_LORA] = jnp.dot(
            q_nope, wuk_ref[hd], preferred_element_type=f32).astype(bf16)
    qrope_ref[...] = _rope_half(qc[:, C_HEADS * C_NOPE:], cosi, silo, sihi).astype(bf16)


def _project(x, lw, tabs, *, tm, decode):
    B, L, _ = x.shape
    grid = (B, L // tm)

    def tok(width, dtype):
        return (pl.BlockSpec((None, tm, width), lambda b, i: (b, i, 0)),
                jax.ShapeDtypeStruct((B, L, width), dtype))

    def tab():
        return pl.BlockSpec((tm, LANES), lambda b, i: (i, 0))

    outs = [tok(A_WIDTH, bf16), tok(A_KV_WIDTH, f32), tok(A_KV_WIDTH, f32), tok(IDX_WIDTH, bf16),
            tok(IDX_DIM, f32), tok(C_ROPE, f32), tok(LANES, f32), tok(B_WIDTH, bf16), tok(KV_LORA, f32),
            tok(C_HEADS * KV_LORA, bf16), tok(C_HEADS * C_ROPE, bf16)]
    if decode:
        outs.append(tok(B_WIDTH, f32))
    weights = [lw['attn_norm'], lw['w_all'], lw['c_q_norm'], lw['c_w_uq'], lw['c_w_uk'], lw['c_kv_norm'],
               lw['b_v_norm'], lw['gate_w'], lw['gate_b']]
    in_specs = ([pl.BlockSpec((None, tm, D_MODEL), lambda b, i: (b, i, 0))]
                + [_const_spec(w.shape) for w in weights] + [tab() for _ in range(5)])
    res = pl.pallas_call(
        functools.partial(_proj_kernel, decode=decode),
        grid=grid, in_specs=in_specs,
        out_specs=[o[0] for o in outs], out_shape=[o[1] for o in outs],
        compiler_params=_params(("parallel", "parallel")),
        name="proj_decode" if decode else "proj_prompt",
    )(x, *weights, *tabs)
    names = ['qa', 'ka', 'va', 'qi', 'ki', 'kr', 'wi', 'ob', 'ckv', 'qlat', 'qrope', 'vn']
    return dict(zip(names, res))


def _dsa_kernel(qi_ref, wi_ref, ki_ref, qa_ref, ka_ref, va_ref, oa_ref, sel_ref, *, topk):
    tq, L = sel_ref.shape
    j = pl.program_id(1)
    ki = ki_ref[...].astype(bf16)
    wi = wi_ref[...]
    score = jnp.zeros((tq, L), f32)
    for hd in range(IDX_HEADS):
        s = lax.dot_general(qi_ref[:, hd * IDX_DIM:(hd + 1) * IDX_DIM], ki, _NT, preferred_element_type=f32)
        score = score + wi[:, hd:hd + 1] * jnp.maximum(s * IDX_DIM ** -0.5, 0.0)
    qpos = j * tq + lax.broadcasted_iota(jnp.int32, (tq, L), 0)
    kpos = lax.broadcasted_iota(jnp.int32, (tq, L), 1)
    causal = kpos <= qpos
    score = jnp.where(causal, score, -jnp.inf)
    _topk_membership(_order_key(score), topk, sel_ref)
    sel = (sel_ref[...] > 0.5) & causal

    ka = ka_ref[...].astype(bf16)
    va = va_ref[...].astype(bf16)
    rep = A_HEADS // A_KV_HEADS
    for hd in range(A_HEADS):
        kv = slice((hd // rep) * A_HEAD_DIM, (hd // rep + 1) * A_HEAD_DIM)
        s = lax.dot_general(qa_ref[:, hd * A_HEAD_DIM:(hd + 1) * A_HEAD_DIM], ka[:, kv], _NT,
                            preferred_element_type=f32) * A_HEAD_DIM ** -0.5
        s = jnp.where(sel, s, -jnp.inf)
        p = jnp.exp(s - jnp.max(s, axis=1, keepdims=True))
        p = p / jnp.sum(p, axis=1, keepdims=True)
        oa_ref[:, hd * A_HEAD_DIM:(hd + 1) * A_HEAD_DIM] = jnp.dot(
            p.astype(bf16), va[:, kv], preferred_element_type=f32).astype(bf16)


def _dsa_prompt(pr, *, tq, topk):
    B, L, _ = pr['qa'].shape

    def qspec(width):
        return pl.BlockSpec((None, tq, width), lambda b, j: (b, j, 0))

    def kspec(width):
        return pl.BlockSpec((None, L, width), lambda b, j: (b, 0, 0))

    return pl.pallas_call(
        functools.partial(_dsa_kernel, topk=topk),
        grid=(B, L // tq),
        in_specs=[qspec(IDX_WIDTH), qspec(LANES), kspec(IDX_DIM), qspec(A_WIDTH), kspec(A_KV_WIDTH),
                  kspec(A_KV_WIDTH)],
        out_specs=qspec(A_WIDTH),
        out_shape=jax.ShapeDtypeStruct((B, L, A_WIDTH), bf16),
        scratch_shapes=[pltpu.VMEM((tq, L), f32)],
        compiler_params=_params(("parallel", "parallel")),
        name="dsa_prompt",
    )(pr['qi'], pr['wi'], pr['ki'], pr['qa'], pr['ka'], pr['va'])


def _mla_kernel(ql_ref, qr_ref, c_ref, kr_ref, ol_ref):
    tq = ql_ref.shape[0]
    L = c_ref.shape[0]
    j = pl.program_id(1)
    c = c_ref[...].astype(bf16)
    kr = kr_ref[...].astype(bf16)
    qpos = j * tq + lax.broadcasted_iota(jnp.int32, (tq, L), 0)
    kpos = lax.broadcasted_iota(jnp.int32, (tq, L), 1)
    causal = kpos <= qpos
    for hd in range(C_HEADS):
        s = (lax.dot_general(ql_ref[:, hd * KV_LORA:(hd + 1) * KV_LORA], c, _NT, preferred_element_type=f32)
             + lax.dot_general(qr_ref[:, hd * C_ROPE:(hd + 1) * C_ROPE], kr, _NT, preferred_element_type=f32))
        s = jnp.where(causal, s * (C_NOPE + C_ROPE) ** -0.5, -jnp.inf)
        p = jnp.exp(s - jnp.max(s, axis=1, keepdims=True))
        p = p / jnp.sum(p, axis=1, keepdims=True)
        ol_ref[:, hd * KV_LORA:(hd + 1) * KV_LORA] = jnp.dot(
            p.astype(bf16), c, preferred_element_type=f32).astype(bf16)


def _mla_prompt(pr, *, tq):
    B, L, _ = pr['qlat'].shape

    def qspec(width):
        return pl.BlockSpec((None, tq, width), lambda b, j: (b, j, 0))

    def kspec(width):
        return pl.BlockSpec((None, L, width), lambda b, j: (b, 0, 0))

    return pl.pallas_call(
        _mla_kernel,
        grid=(B, L // tq),
        in_specs=[qspec(C_HEADS * KV_LORA), qspec(C_HEADS * C_ROPE), kspec(KV_LORA), kspec(C_ROPE)],
        out_specs=qspec(C_HEADS * KV_LORA),
        out_shape=jax.ShapeDtypeStruct((B, L, C_HEADS * KV_LORA), bf16),
        compiler_params=_params(("parallel", "parallel")),
        name="mla_prompt",
    )(pr['qlat'], pr['qrope'], pr['ckv'], pr['kr'])


def _mix_kernel(x_ref, oa_ref, ob_ref, ol_ref, gain_ref, wg_ref, wa_ref, wb_ref, wc_ref, wuv_ref, wo_ref, x1_ref):
    x = x_ref[...]
    h = _rms(x, gain_ref[...]).astype(bf16)
    oc = jnp.concatenate(
        [jnp.dot(ol_ref[:, hd * KV_LORA:(hd + 1) * KV_LORA], wuv_ref[hd], preferred_element_type=f32)
         for hd in range(C_HEADS)], axis=1).astype(bf16)
    merged = None
    for k, (o, w_ref) in enumerate(((oa_ref[...], wa_ref), (ob_ref[...], wb_ref), (oc, wc_ref))):
        gate = jax.nn.sigmoid(jnp.dot(h, wg_ref[:, k * D_MODEL:(k + 1) * D_MODEL], preferred_element_type=f32))
        term = gate * jnp.dot(o, w_ref[...], preferred_element_type=f32)
        merged = term if merged is None else merged + term
    x1_ref[...] = x + jnp.dot(merged.astype(bf16), wo_ref[...], preferred_element_type=f32)


def _mixer(x, oa, ob, ol, lw, *, tm):
    B, L, _ = x.shape

    def tok(width):
        return pl.BlockSpec((None, tm, width), lambda b, i: (b, i, 0))

    weights = [lw['attn_norm'], lw['w_gate'], lw['w_br_a'], lw['w_br_b'], lw['w_br_c'], lw['c_w_uv'], lw['w_out']]
    return pl.pallas_call(
        _mix_kernel,
        grid=(B, L // tm),
        in_specs=[tok(D_MODEL), tok(A_WIDTH), tok(B_WIDTH), tok(C_HEADS * KV_LORA)]
        + [_const_spec(w.shape) for w in weights],
        out_specs=tok(D_MODEL),
        out_shape=jax.ShapeDtypeStruct((B, L, D_MODEL), f32),
        compiler_params=_params(("parallel", "parallel")),
        name="mixer",
    )(x, oa, ob, ol, *weights)


def _ffn_kernel(*refs, decode, final):
    if decode:
        x1_ref, gain_ref, wa_ref, wg_ref, cw_ref, cb_ref, wd_ref, fin_ref, h0_ref, h1_ref = refs[:10]
        outs = refs[10:]
    else:
        x1_ref, gain_ref, wa_ref, wg_ref, cw_ref, cb_ref, wd_ref, fin_ref = refs[:8]
        outs = refs[8:]
    x2_ref, hist_ref = outs[0], outs[1]
    tm = x1_ref.shape[0]
    x1 = x1_ref[...]
    h = _rms(x1, gain_ref[...]).astype(bf16)
    a = jnp.dot(h, wa_ref[...], preferred_element_type=f32)
    g = jnp.dot(h, wg_ref[...], preferred_element_type=f32)
    cw = cw_ref[...]
    if decode:
        hist_ref[...] = g
        conv = g * cw[2:3] + cb_ref[...] + h0_ref[...] * cw[0:1] + h1_ref[...] * cw[1:2]
    else:
        gbuf = outs[-1]

        @pl.when(pl.program_id(1) == 0)
        def _():
            gbuf[0:SUBLANES, :] = jnp.zeros((SUBLANES, D_FF), f32)

        gbuf[SUBLANES:SUBLANES + tm, :] = g
        conv = g * cw[2:3] + cb_ref[...]
        for k in range(CONV_W - 1):
            off = SUBLANES - (CONV_W - 1) + k
            conv = conv + gbuf[off:off + tm, :] * cw[k:k + 1]
        hist_ref[...] = g[tm - (CONV_W - 1):, :]
        gbuf[0:SUBLANES, :] = g[tm - SUBLANES:, :]
    y = jnp.dot((jax.nn.silu(conv) * a).astype(bf16), wd_ref[...], preferred_element_type=f32)
    x2 = x1 + y
    x2_ref[...] = x2
    if final:
        outs[2][...] = _rms(x2, fin_ref[...])


def _ffn(x1, lw, final_norm, *, tm, final, hist=None):
    B, L, _ = x1.shape
    decode = hist is not None

    def tok(width):
        return pl.BlockSpec((None, tm, width), lambda b, i: (b, i, 0))

    weights = [lw['ffn_norm'], lw['ffn_w_a'], lw['ffn_w_g'], lw['ffn_conv_w'], lw['ffn_conv_b'], lw['ffn_w_down'],
               final_norm]
    in_specs = [tok(D_MODEL)] + [_const_spec(w.shape) for w in weights]
    args = [x1, *weights]
    out_specs = [tok(D_MODEL)]
    out_shape = [jax.ShapeDtypeStruct((B, L, D_MODEL), f32)]
    scratch = []
    if decode:
        in_specs += [tok(D_FF), tok(D_FF)]
        args += list(hist)
        out_specs.append(tok(D_FF))
        out_shape.append(jax.ShapeDtypeStruct((B, L, D_FF), f32))
    else:
        out_specs.append(pl.BlockSpec((None, CONV_W - 1, D_FF), lambda b, i: (b, 0, 0)))
        out_shape.append(jax.ShapeDtypeStruct((B, CONV_W - 1, D_FF), f32))
        scratch.append(pltpu.VMEM((tm + SUBLANES, D_FF), f32))
    if final:
        out_specs.append(tok(D_MODEL))
        out_shape.append(jax.ShapeDtypeStruct((B, L, D_MODEL), f32))
    return pl.pallas_call(
        functools.partial(_ffn_kernel, decode=decode, final=final),
        grid=(B, L // tm), in_specs=in_specs, out_specs=out_specs, out_shape=out_shape,
        scratch_shapes=scratch,
        compiler_params=_params(("parallel", "arbitrary")),
        name="ffn_decode" if decode else "ffn_prompt",
    )(*args)


def _sweep1_kernel(pt_ref, qi_ref, wi_ref, ql_ref, qr_ref, kin_ref, cn_ref, krn_ref, *rest, pages, n_steps):
    idx_refs, lat_refs, kro_refs = rest[:pages], rest[pages:2 * pages], rest[2 * pages:3 * pages]
    sc_ref, ol_ref, m_ref, l_ref, acc_ref = rest[3 * pages:]
    s = pl.program_id(1)
    scale_c = (C_NOPE + C_ROPE) ** -0.5

    @pl.when(s == 0)
    def _():
        m_ref[...] = jnp.full(m_ref.shape, -jnp.inf, f32)
        l_ref[...] = jnp.zeros(l_ref.shape, f32)
        acc_ref[...] = jnp.zeros(acc_ref.shape, f32)

    def online(sm, values):
        m_old = m_ref[...]
        m_new = jnp.maximum(m_old, jnp.max(sm, axis=1, keepdims=True))
        alpha = jnp.exp(m_old - m_new)
        p = jnp.exp(sm - m_new)
        l_ref[...] = alpha * l_ref[...] + jnp.sum(p, axis=1, keepdims=True)
        acc_ref[...] = alpha * acc_ref[...] + values(p)
        m_ref[...] = m_new

    @pl.when(s < n_steps)
    def _():
        qi, wi = qi_ref[...], wi_ref[...]
        ql, qr = ql_ref[...], qr_ref[...]
        sc_parts, sm_parts, lat = [], [], []
        for i in range(pages):
            kidx = idx_refs[i][...].astype(bf16)
            si = lax.dot_general(qi, kidx, _NT, preferred_element_type=f32)
            sc_parts.append(jnp.sum(wi * jnp.maximum(si * IDX_DIM ** -0.5, 0.0), axis=0, keepdims=True))
            ci = lat_refs[i][...].astype(bf16)
            lat.append(ci)
            sm_parts.append((lax.dot_general(ql, ci, _NT, preferred_element_type=f32)
                             + lax.dot_general(qr, kro_refs[i][...].astype(bf16), _NT,
                                               preferred_element_type=f32)) * scale_c)
        sc_ref[...] = jnp.concatenate(sc_parts, axis=1)

        def values(p):
            pb = p.astype(bf16)
            out = None
            for i in range(pages):
                t = jnp.dot(pb[:, i * PAGE_SIZE:(i + 1) * PAGE_SIZE], lat[i], preferred_element_type=f32)
                out = t if out is None else out + t
            return out

        online(jnp.concatenate(sm_parts, axis=1), values)

    @pl.when(s == n_steps)
    def _():
        s_new = jnp.sum(qi_ref[...].astype(f32) * kin_ref[...], axis=1, keepdims=True)
        sc_new = jnp.sum(wi_ref[...] * jnp.maximum(s_new * IDX_DIM ** -0.5, 0.0), axis=0, keepdims=True)
        lane = lax.broadcasted_iota(jnp.int32, sc_ref.shape, 1)
        sc_ref[...] = jnp.where(lane == 0, sc_new, -jnp.inf)
        cn = cn_ref[...]
        krn = krn_ref[...]
        sm = (jnp.sum(ql_ref[...].astype(f32) * cn, axis=1, keepdims=True)
              + jnp.sum(qr_ref[...].astype(f32) * krn, axis=1, keepdims=True)) * scale_c
        online(sm, lambda p: p * cn)
        ol_ref[...] = acc_ref[...] / l_ref[...]


def _sweep1(sr, caches, page_table, layer, *, pages):
    qi, wi, ql, qr, kin, cn, krn = sr
    cache_idx, cache_lat, cache_kro = caches
    S, n_pages = page_table.shape
    n_steps = n_pages // pages
    width = pages * PAGE_SIZE

    def per_seq(a):
        nd = a.ndim - 1
        return pl.BlockSpec((None,) + a.shape[1:], lambda b, s, pt: (b,) + (0,) * nd)

    def page_spec(cache, i):
        return pl.BlockSpec(
            (None, None) + cache.shape[2:],
            lambda b, s, pt: (layer, pt[b, jnp.minimum(s, n_steps - 1) * pages + i], 0, 0))

    in_specs = [per_seq(a) for a in (qi, wi, ql, qr, kin, cn, krn)]
    for cache in (cache_idx, cache_lat, cache_kro):
        in_specs += [page_spec(cache, i) for i in range(pages)]
    grid_spec = pltpu.PrefetchScalarGridSpec(
        num_scalar_prefetch=1, grid=(S, n_steps + 1), in_specs=in_specs,
        out_specs=[pl.BlockSpec((None, None, 1, width), lambda b, s, pt: (b, s, 0, 0)),
                   pl.BlockSpec((None, _ROW_PAD, KV_LORA), lambda b, s, pt: (b, 0, 0))],
        scratch_shapes=[pltpu.VMEM((_ROW_PAD, 1), f32), pltpu.VMEM((_ROW_PAD, 1), f32),
                        pltpu.VMEM((_ROW_PAD, KV_LORA), f32)])
    return pl.pallas_call(
        functools.partial(_sweep1_kernel, pages=pages, n_steps=n_steps),
        grid_spec=grid_spec,
        out_shape=[jax.ShapeDtypeStruct((S, n_steps + 1, 1, width), f32),
                   jax.ShapeDtypeStruct((S, _ROW_PAD, KV_LORA), f32)],
        compiler_params=_params(("parallel", "arbitrary")),
        name="decode_indexer_mla",
    )(page_table, qi, wi, ql, qr, kin, cn, krn,
      *([cache_idx] * pages), *([cache_lat] * pages), *([cache_kro] * pages))


def _select_kernel(sc_ref, sel_ref, *, topk):
    _topk_membership(_order_key(sc_ref[...]), topk, sel_ref)


def _select(scores, *, topk):
    return pl.pallas_call(
        functools.partial(_select_kernel, topk=topk),
        out_shape=jax.ShapeDtypeStruct(scores.shape, f32),
        compiler_params=pltpu.CompilerParams(vmem_limit_bytes=VMEM_LIMIT),
        name="decode_select",
    )(scores)


def _sweep2_kernel(pt_ref, qa_ref, kn_ref, vn_ref, sel_ref, *rest, pages, n_steps):
    k_refs, v_refs = rest[:pages], rest[pages:2 * pages]
    oa_ref, m_ref, l_ref, acc_ref = rest[2 * pages:]
    s = pl.program_id(1)
    rep = A_HEADS // A_KV_HEADS
    row = lax.broadcasted_iota(jnp.int32, (_ROW_PAD, 1), 0)
    first_group = row < rep
    scale = A_HEAD_DIM ** -0.5

    @pl.when(s == 0)
    def _():
        m_ref[...] = jnp.full(m_ref.shape, -jnp.inf, f32)
        l_ref[...] = jnp.zeros(l_ref.shape, f32)
        acc_ref[...] = jnp.zeros(acc_ref.shape, f32)

    def online(sm, keep, values):
        m_old = m_ref[...]
        m_new = jnp.maximum(m_old, jnp.max(jnp.where(keep, sm, -jnp.inf), axis=1, keepdims=True))
        m_use = jnp.where(m_new == -jnp.inf, 0.0, m_new)
        alpha = jnp.exp(m_old - m_use)
        p = jnp.where(keep, jnp.exp(sm - m_use), 0.0)
        l_ref[...] = alpha * l_ref[...] + jnp.sum(p, axis=1, keepdims=True)
        acc_ref[...] = alpha * acc_ref[...] + values(p)
        m_ref[...] = m_new

    @pl.when(s < n_steps)
    def _():
        q = qa_ref[...]
        parts, vals = [], []
        for i in range(pages):
            k = k_refs[i][...].astype(bf16)
            vals.append(v_refs[i][...].astype(bf16))
            s0 = lax.dot_general(q, k[:, :A_HEAD_DIM], _NT, preferred_element_type=f32)
            s1 = lax.dot_general(q, k[:, A_HEAD_DIM:], _NT, preferred_element_type=f32)
            parts.append(jnp.where(first_group, s0, s1) * scale)

        def values(p):
            pb = p.astype(bf16)
            out = None
            for i in range(pages):
                pi = pb[:, i * PAGE_SIZE:(i + 1) * PAGE_SIZE]
                t = jnp.where(first_group,
                              jnp.dot(pi, vals[i][:, :A_HEAD_DIM], preferred_element_type=f32),
                              jnp.dot(pi, vals[i][:, A_HEAD_DIM:], preferred_element_type=f32))
                out = t if out is None else out + t
            return out

        online(jnp.concatenate(parts, axis=1), sel_ref[...] > 0.5, values)

    @pl.when(s == n_steps)
    def _():
        q = qa_ref[...].astype(f32)
        kn = kn_ref[...]
        vn = vn_ref[...]
        k_row = jnp.where(first_group, kn[:, :A_HEAD_DIM], kn[:, A_HEAD_DIM:])
        v_row = jnp.where(first_group, vn[:, :A_HEAD_DIM], vn[:, A_HEAD_DIM:])
        sm = jnp.sum(q * k_row, axis=1, keepdims=True) * scale
        online(sm, sel_ref[:, 0:1] > 0.5, lambda p: p * v_row)
        oa_ref[...] = acc_ref[...] / l_ref[...]


def _sweep2(qa, kn, vn, sel, cache_k, cache_v, page_table, layer, *, pages):
    S, n_pages = page_table.shape
    n_steps = n_pages // pages
    width = pages * PAGE_SIZE

    def per_seq(a):
        nd = a.ndim - 1
        return pl.BlockSpec((None,) + a.shape[1:], lambda b, s, pt: (b,) + (0,) * nd)

    def page_spec(cache, i):
        return pl.BlockSpec(
            (None, None) + cache.shape[2:],
            lambda b, s, pt: (layer, pt[b, jnp.minimum(s, n_steps - 1) * pages + i], 0, 0))

    in_specs = [per_seq(qa), per_seq(kn), per_seq(vn),
                pl.BlockSpec((None, None, 1, width), lambda b, s, pt: (b, s, 0, 0))]
    for cache in (cache_k, cache_v):
        in_specs += [page_spec(cache, i) for i in range(pages)]
    grid_spec = pltpu.PrefetchScalarGridSpec(
        num_scalar_prefetch=1, grid=(S, n_steps + 1), in_specs=in_specs,
        out_specs=pl.BlockSpec((None, _ROW_PAD, A_HEAD_DIM), lambda b, s, pt: (b, 0, 0)),
        scratch_shapes=[pltpu.VMEM((_ROW_PAD, 1), f32), pltpu.VMEM((_ROW_PAD, 1), f32),
                        pltpu.VMEM((_ROW_PAD, A_HEAD_DIM), f32)])
    return pl.pallas_call(
        functools.partial(_sweep2_kernel, pages=pages, n_steps=n_steps),
        grid_spec=grid_spec,
        out_shape=jax.ShapeDtypeStruct((S, _ROW_PAD, A_HEAD_DIM), f32),
        compiler_params=_params(("parallel", "arbitrary")),
        name="decode_dsa",
    )(page_table, qa, kn, vn, sel, *([cache_k] * pages), *([cache_v] * pages))


def _rope_tables(pos):
    def cs(half):
        inv = ROPE_THETA ** (-jnp.arange(half, dtype=f32) / half)
        ang = pos.astype(f32)[:, None] * inv[None, :]
        return jnp.cos(ang), jnp.sin(ang)

    c, s = cs(A_HEAD_DIM // 2)
    cos_a = jnp.concatenate([c, c], axis=1)
    sin_a = jnp.concatenate([-s, s], axis=1)
    c, s = cs(IDX_DIM // 2)
    z = jnp.zeros_like(s)
    cos_i = jnp.concatenate([c, c, c, c], axis=1)
    sin_lo = jnp.concatenate([-s, z, -s, z], axis=1)
    sin_hi = jnp.concatenate([z, s, z, s], axis=1)
    return cos_a, sin_a, cos_i, sin_lo, sin_hi


def _layer_weights(l, w_in, attn_norm, c_q_norm, c_w_uq, c_kv_norm, c_w_uk, c_w_uv, b_v_norm, b_w_s, b_bias,
                   w_br_a, w_br_b, w_br_c, w_out, ffn_norm, ffn_w_up, ffn_conv_w, ffn_conv_b, ffn_w_down):
    w = w_in[l]
    o_ki = A_WIDTH + 2 * A_KV_WIDTH + IDX_WIDTH
    o_wi = o_ki + IDX_DIM
    o_u = o_wi + IDX_HEADS
    o_cq = o_u + 2 * B_WIDTH
    o_kr = o_cq + Q_LORA + KV_LORA
    o_gl = o_kr + C_ROPE
    w_all = jnp.concatenate(
        [w[:, :o_ki], w[:, o_ki:o_wi], w[:, o_kr:o_gl],
         jnp.pad(w[:, o_wi:o_u], ((0, 0), (0, LANES - IDX_HEADS))), w[:, o_u:o_cq], w[:, o_cq:o_kr]], axis=1)
    uq = c_w_uq[l]
    return {
        'attn_norm': attn_norm[l][None], 'w_all': w_all.astype(bf16), 'w_gate': w[:, o_gl:].astype(bf16),
        'c_q_norm': c_q_norm[l][None], 'c_kv_norm': c_kv_norm[l][None], 'b_v_norm': b_v_norm[l][None],
        'c_w_uq': jnp.concatenate([uq[:, :, :C_NOPE].reshape(Q_LORA, C_HEADS * C_NOPE),
                                   uq[:, :, C_NOPE:].reshape(Q_LORA, C_HEADS * C_ROPE)], axis=1).astype(bf16),
        'c_w_uk': jnp.transpose(c_w_uk[l], (1, 2, 0)).astype(bf16),
        'c_w_uv': jnp.transpose(c_w_uv[l], (1, 0, 2)).astype(bf16),
        'b_w_s': b_w_s[l], 'b_bias_t': b_bias[l].T,
        'b_w_00': jnp.repeat(b_w_s[l][:, 0, 0], B_GROUP_DIM)[None], 'b_bias_0': jnp.repeat(b_bias[l][:, 0], B_GROUP_DIM)[None],
        'w_br_a': w_br_a[l].astype(bf16), 'w_br_b': w_br_b[l].astype(bf16), 'w_br_c': w_br_c[l].astype(bf16),
        'w_out': w_out[l].astype(bf16), 'ffn_norm': ffn_norm[l][None],
        'ffn_w_a': ffn_w_up[l][:, :D_FF].astype(bf16), 'ffn_w_g': ffn_w_up[l][:, D_FF:].astype(bf16),
        'ffn_conv_w': ffn_conv_w[l], 'ffn_conv_b': ffn_conv_b[l][None], 'ffn_w_down': ffn_w_down[l].astype(bf16),
    }


def _pad_rows(a):
    return jnp.pad(a, ((0, 0), (0, _ROW_PAD - a.shape[1]), (0, 0)))


TM_PROJ = 256
TM_MIX = 256
TM_FFN = 256
TQ_DSA = 128
TQ_MLA = 128
SWEEP_PAGES = 8


def kernel(x_prompt, x_sample, cache_a_k, cache_a_v, cache_a_idxk, cache_c_latent, cache_c_krope, state_ffn_conv, page_table, attn_norm, w_in, c_q_norm, c_w_uq, c_kv_norm, c_w_uk, c_w_uv, b_v_norm, b_w_s, b_bias, w_br_a, w_br_b, w_br_c, w_out, ffn_norm, ffn_w_up, ffn_conv_w, ffn_conv_b, ffn_w_down, final_norm):
    bsz, seq, _ = x_prompt.shape
    dbsz, dseq, _ = x_sample.shape
    depth = w_in.shape[0]
    assert dseq == 1 and seq % CHUNK == 0
    past_len = page_table.shape[1] * PAGE_SIZE
    assert past_len % CHUNK == 0
    topk_p = min(TOPK_MAX, seq // 4)
    topk_s = min(TOPK_MAX, (past_len + dseq) // 4)
    tabs_p = _rope_tables(jnp.arange(seq, dtype=jnp.int32))
    tabs_s = _rope_tables(jnp.full((dbsz,), past_len, dtype=jnp.int32))
    fin = final_norm[None]
    cache_k2 = cache_a_k.reshape(cache_a_k.shape[:3] + (A_KV_WIDTH,))
    cache_v2 = cache_a_v.reshape(cache_a_v.shape[:3] + (A_KV_WIDTH,))

    xp = x_prompt
    xs = x_sample.reshape(1, dbsz, D_MODEL)
    outs = {k: [] for k in ('pk', 'pv', 'pik', 'pcl', 'pcr', 'pfc', 'sk', 'sv', 'sik', 'scl', 'scr', 'sfc', 'sbv')}
    yp = ys = None
    for l in range(depth):
        lw = _layer_weights(l, w_in, attn_norm, c_q_norm, c_w_uq, c_kv_norm, c_w_uk, c_w_uv, b_v_norm, b_w_s,
                            b_bias, w_br_a, w_br_b, w_br_c, w_out, ffn_norm, ffn_w_up, ffn_conv_w, ffn_conv_b,
                            ffn_w_down)
        final = l == depth - 1

        pr = _project(xp, dict(lw, gate_w=lw['b_w_s'], gate_b=lw['b_bias_t']), tabs_p, tm=TM_PROJ, decode=False)
        oa = _dsa_prompt(pr, tq=TQ_DSA, topk=topk_p)
        ol = _mla_prompt(pr, tq=TQ_MLA)
        x1 = _mixer(xp, oa, pr['ob'], ol, lw, tm=TM_MIX)
        res = _ffn(x1, lw, fin, tm=TM_FFN, final=final)
        xp = res[0]
        if final:
            yp = res[2]
        outs['pk'].append(pr['ka'].reshape(bsz, seq, A_KV_HEADS, A_HEAD_DIM))
        outs['pv'].append(pr['va'].reshape(bsz, seq, A_KV_HEADS, A_HEAD_DIM))
        outs['pik'].append(pr['ki']); outs['pcl'].append(pr['ckv']); outs['pcr'].append(pr['kr'])
        outs['pfc'].append(res[1])

        sr = _project(xs, dict(lw, gate_w=lw['b_w_00'], gate_b=lw['b_bias_0']), tabs_s, tm=dbsz, decode=True)
        seqs = lambda a, r: a.reshape(dbsz, r, a.shape[-1] // r)
        sweep_in = (_pad_rows(seqs(sr['qi'], IDX_HEADS)),
                    _pad_rows(sr['wi'][0, :, :IDX_HEADS, None]),
                    _pad_rows(seqs(sr['qlat'], C_HEADS)), _pad_rows(seqs(sr['qrope'], C_HEADS)),
                    seqs(sr['ki'], 1), seqs(sr['ckv'], 1), seqs(sr['kr'], 1))
        scores, ol_s = _sweep1(sweep_in, (cache_a_idxk, cache_c_latent, cache_c_krope), page_table, l,
                               pages=SWEEP_PAGES)
        sel = _select(scores.reshape(dbsz, -1), topk=topk_s).reshape(scores.shape)
        oa_s = _sweep2(_pad_rows(seqs(sr['qa'], A_HEADS)), seqs(sr['ka'], 1), seqs(sr['va'], 1), sel,
                       cache_k2, cache_v2, page_table, l, pages=SWEEP_PAGES)
        oa_s = oa_s[:, :A_HEADS].reshape(1, dbsz, A_WIDTH).astype(bf16)
        ol_s = ol_s[:, :C_HEADS].reshape(1, dbsz, C_HEADS * KV_LORA).astype(bf16)
        x1s = _mixer(xs, oa_s, sr['ob'], ol_s, lw, tm=dbsz)
        hist = state_ffn_conv[l]
        res_s = _ffn(x1s, lw, fin, tm=dbsz, final=final,
                     hist=(hist[None, :, 0], hist[None, :, 1]))
        xs = res_s[0]
        if final:
            ys = res_s[2]
        outs['sk'].append(sr['ka'].reshape(dbsz, dseq, A_KV_HEADS, A_HEAD_DIM))
        outs['sv'].append(sr['va'].reshape(dbsz, dseq, A_KV_HEADS, A_HEAD_DIM))
        outs['sik'].append(sr['ki'].reshape(dbsz, dseq, IDX_DIM))
        outs['scl'].append(sr['ckv'].reshape(dbsz, dseq, KV_LORA))
        outs['scr'].append(sr['kr'].reshape(dbsz, dseq, C_ROPE))
        outs['sfc'].append(jnp.stack([hist[:, 1], res_s[1][0]], axis=1))
        outs['sbv'].append(sr['vn'].reshape(dbsz, dseq, B_WIDTH))

    st = {k: jnp.stack(v) for k, v in outs.items()}
    return (yp, ys.reshape(dbsz, dseq, D_MODEL), st['pk'], st['pv'], st['pik'], st['pcl'], st['pcr'], st['pfc'],
            st['sk'], st['sv'], st['sik'], st['scl'], st['scr'], st['sfc'], st['sbv'])
```

```python
import functools

import jax
import jax.numpy as jnp
from jax import lax
from jax.experimental import pallas as pl
from jax.experimental.pallas import tpu as pltpu

D_MODEL = 1024
PAGE_SIZE = 128
A_HEADS = 4
A_KV_HEADS = 2
A_HEAD_DIM = 128
A_WIDTH = A_HEADS * A_HEAD_DIM
A_KV_WIDTH = A_KV_HEADS * A_HEAD_DIM
IDX_HEADS = 8
IDX_DIM = 64
IDX_WIDTH = IDX_HEADS * IDX_DIM
TOPK_MAX = 256
CHUNK = 128
B_GROUPS = 4
B_GROUP_DIM = 128
B_WIDTH = B_GROUPS * B_GROUP_DIM
C_HEADS = 4
Q_LORA = 256
KV_LORA = 256
C_NOPE = 128
C_ROPE = 64
C_VDIM = 128
C_WIDTH = C_HEADS * C_VDIM
N_BRANCH = 3
D_FF = 2816
CONV_W = 3
ROPE_THETA = 10000.0
EPS = 1e-6

LANES = 128
SUBLANES = 8
VMEM_LIMIT = 56 * 1024 * 1024

_O_QA = 0
_O_KA = _O_QA + A_WIDTH
_O_VA = _O_KA + A_KV_WIDTH
_O_QI = _O_VA + A_KV_WIDTH
_O_KIKR = _O_QI + IDX_WIDTH
_O_WI = _O_KIKR + LANES
_O_U = _O_WI + LANES
_O_V = _O_U + B_WIDTH
_O_CQ = _O_V + B_WIDTH
_O_CKV = _O_CQ + Q_LORA
_W_ALL = _O_CKV + KV_LORA

_NT = (((1,), (1,)), ((), ()))
_INT_MIN = -2147483648
_NEG_INF_KEY = -2139095041
_ROW_PAD = 16
_SEARCH_GROUP_ROWS = 64

f32 = jnp.float32
bf16 = jnp.bfloat16


def _rms(x, g):
    return x * lax.rsqrt(jnp.mean(x * x, axis=-1, keepdims=True) + EPS) * g


def _rope_full(x, cos, sin):
    parts = []
    for k in range(x.shape[1] // LANES):
        xs = x[:, k * LANES:(k + 1) * LANES]
        parts.append(xs * cos + pltpu.roll(xs, LANES // 2, 1) * sin)
    return parts[0] if len(parts) == 1 else jnp.concatenate(parts, axis=1)


def _rope_half(x, cos, sin_lo, sin_hi):
    parts = []
    for k in range(x.shape[1] // LANES):
        xs = x[:, k * LANES:(k + 1) * LANES]
        parts.append(xs * cos + pltpu.roll(xs, LANES - IDX_DIM // 2, 1) * sin_lo
                     + pltpu.roll(xs, IDX_DIM // 2, 1) * sin_hi)
    return parts[0] if len(parts) == 1 else jnp.concatenate(parts, axis=1)


def _order_key(score):
    bits = lax.bitcast_convert_type(score, jnp.int32)
    return bits ^ ((bits >> 31) & jnp.int32(0x7FFFFFFF))


def _count(mask):
    return jnp.sum(jnp.where(mask, 1.0, 0.0), axis=1, keepdims=True)


def _kth_largest_key(key, k):
    rows = key.shape[0]
    group = min(rows, _SEARCH_GROUP_ROWS)
    parts = [key[r:r + group] for r in range(0, rows, group)]

    def body(i, tbs):
        bit = jnp.left_shift(jnp.int32(1), 31 - i)
        out = []
        for part, tb in zip(parts, tbs):
            cand_b = tb | bit
            cnt = _count(part >= (cand_b ^ jnp.int32(_INT_MIN)))
            out.append(jnp.where(cnt >= k, cand_b, tb))
        return tuple(out)

    tbs = lax.fori_loop(0, 32, body, tuple(jnp.zeros((group, 1), jnp.int32) for _ in parts))
    tb = tbs[0] if len(tbs) == 1 else jnp.concatenate(tbs, axis=0)
    return tb ^ jnp.int32(_INT_MIN)


def _topk_membership(key, k, sel_ref):
    rows, n = key.shape
    thr = _kth_largest_key(key, k)
    gt = key > thr
    eq = key == thr
    need = k - _count(gt)
    sel_ref[...] = jnp.where(gt | eq, 1.0, 0.0)
    tie = (thr > _NEG_INF_KEY) & (_count(eq) > need)

    @pl.when(jnp.max(jnp.where(tie, 1.0, 0.0)) > 0.0)
    def _():
        r = lax.broadcasted_iota(jnp.int32, (LANES, LANES), 0)
        c = lax.broadcasted_iota(jnp.int32, (LANES, LANES), 1)
        upper = jnp.where(r < c, 1.0, 0.0).astype(bf16)
        run = jnp.zeros((rows, 1), f32)
        for j in range(n // LANES):
            sl = slice(j * LANES, (j + 1) * LANES)
            eq_j = jnp.where(eq[:, sl], 1.0, 0.0)
            before = jnp.dot(eq_j.astype(bf16), upper, preferred_element_type=f32) + run
            keep = gt[:, sl] | (eq[:, sl] & (before < need))
            sel_ref[:, sl] = jnp.where(keep, 1.0, 0.0)
            run = run + jnp.sum(eq_j, axis=1, keepdims=True)


def _softmax_rows(s):
    p = jnp.exp(s - jnp.max(s, axis=1, keepdims=True))
    return p / jnp.sum(p, axis=1, keepdims=True)


def _const_spec(shape):
    nd = len(shape)
    return pl.BlockSpec(shape, lambda *_: (0,) * nd, pipeline_mode=pl.Buffered(1))


def _params(sem):
    return pltpu.CompilerParams(dimension_semantics=sem, vmem_limit_bytes=VMEM_LIMIT)


def _causal_buckets(j, tq, L, n_buckets, body):
    step = L // n_buckets
    per = step // tq
    for bk in range(n_buckets):
        @pl.when((j >= bk * per) & (j < (bk + 1) * per))
        def _(bk=bk):
            body((bk + 1) * step)


_PROJ_OUTS = (('qa', A_WIDTH, bf16), ('ka', A_KV_WIDTH, f32), ('va', A_KV_WIDTH, f32), ('qi', IDX_WIDTH, bf16),
              ('ki', IDX_DIM, f32), ('kr', C_ROPE, f32), ('wi', LANES, f32), ('ob', B_WIDTH, bf16),
              ('ckv', KV_LORA, f32), ('qlat', C_HEADS * KV_LORA, bf16), ('qrope', C_HEADS * C_ROPE, bf16))
_PROJ_OUTS_PROMPT = (('ka_b', A_KV_WIDTH, bf16), ('va_b', A_KV_WIDTH, bf16), ('ki_b', IDX_DIM, bf16),
                     ('ckv_b', KV_LORA, bf16), ('kr_b', C_ROPE, bf16))
_PROJ_OUTS_DECODE = (('vn', B_WIDTH, f32),)


def _proj_kernel(x_ref, gain_ref, w_ref, cqn_ref, wuq_ref, wuk_ref, ckvn_ref, bvn_ref, ws_ref, bias_ref,
                 cosa_ref, sina_ref, cosi_ref, silo_ref, sihi_ref, *out_refs, decode):
    names = [n for n, _, _ in _PROJ_OUTS + (_PROJ_OUTS_DECODE if decode else _PROJ_OUTS_PROMPT)]
    o = dict(zip(names, out_refs))
    tm = x_ref.shape[0]
    h = _rms(x_ref[...], gain_ref[...]).astype(bf16)
    z = jnp.dot(h, w_ref[...], preferred_element_type=f32)
    cosa, sina = cosa_ref[...], sina_ref[...]
    cosi, silo, sihi = cosi_ref[...], silo_ref[...], sihi_ref[...]

    o['qa'][...] = _rope_full(z[:, _O_QA:_O_KA], cosa, sina).astype(bf16)
    ka = _rope_full(z[:, _O_KA:_O_VA], cosa, sina)
    va = z[:, _O_VA:_O_QI]
    o['ka'][...] = ka
    o['va'][...] = va
    o['qi'][...] = _rope_half(z[:, _O_QI:_O_KIKR], cosi, silo, sihi).astype(bf16)
    kikr = _rope_half(z[:, _O_KIKR:_O_WI], cosi, silo, sihi)
    o['ki'][...] = kikr[:, :IDX_DIM]
    o['kr'][...] = kikr[:, IDX_DIM:]
    o['wi'][...] = z[:, _O_WI:_O_U] * IDX_HEADS ** -0.5
    ckv = _rms(z[:, _O_CKV:_W_ALL], ckvn_ref[...])
    o['ckv'][...] = ckv
    if not decode:
        o['ka_b'][...] = ka.astype(bf16)
        o['va_b'][...] = va.astype(bf16)
        kikr_b = kikr.astype(bf16)
        o['ki_b'][...] = kikr_b[:, :IDX_DIM]
        o['kr_b'][...] = kikr_b[:, IDX_DIM:]
        o['ckv_b'][...] = ckv.astype(bf16)

    u = jax.nn.gelu(z[:, _O_U:_O_V])
    v = _rms(jax.nn.gelu(z[:, _O_V:_O_CQ]), bvn_ref[...])
    if decode:
        o['vn'][...] = v
        o['ob'][...] = (u * (ws_ref[...] * v + bias_ref[...])).astype(bf16)
    else:
        r = lax.broadcasted_iota(jnp.int32, (CHUNK, CHUNK), 0)
        c = lax.broadcasted_iota(jnp.int32, (CHUNK, CHUNK), 1)
        vb = v.astype(bf16)
        bias = bias_ref[...]
        for g in range(B_GROUPS):
            w_g = jnp.where(r >= c, ws_ref[g], 0.0).astype(bf16)
            cols = slice(g * B_GROUP_DIM, (g + 1) * B_GROUP_DIM)
            for k in range(tm // CHUNK):
                rows = slice(k * CHUNK, (k + 1) * CHUNK)
                mixed = jnp.dot(w_g, vb[rows, cols], preferred_element_type=f32) + bias[:, g:g + 1]
                o['ob'][rows, cols] = (u[rows, cols] * mixed).astype(bf16)

    cq = _rms(z[:, _O_CQ:_O_CKV], cqn_ref[...]).astype(bf16)
    qc = jnp.dot(cq, wuq_ref[...], preferred_element_type=f32)
    for hd in range(C_HEADS):
        q_nope = qc[:, hd * C_NOPE:(hd + 1) * C_NOPE].astype(bf16)
        o['qlat'][:, hd * KV_LORA:(hd + 1) * KV_LORA] = jnp.dot(
            q_nope, wuk_ref[hd], preferred_element_type=f32).astype(bf16)
    o['qrope'][...] = _rope_half(qc[:, C_HEADS * C_NOPE:], cosi, silo, sihi).astype(bf16)


def _project(x, lw, tabs, *, tm, decode):
    B, L, _ = x.shape
    outs = _PROJ_OUTS + (_PROJ_OUTS_DECODE if decode else _PROJ_OUTS_PROMPT)

    def tok(width):
        return pl.BlockSpec((None, tm, width), lambda b, i: (b, i, 0))

    weights = [lw['attn_norm'], lw['w_all'], lw['c_q_norm'], lw['c_w_uq'], lw['c_w_uk'], lw['c_kv_norm'],
               lw['b_v_norm'], lw['gate_w'], lw['gate_b']]
    in_specs = ([tok(D_MODEL)] + [_const_spec(w.shape) for w in weights]
                + [pl.BlockSpec((tm, LANES), lambda b, i: (i, 0)) for _ in range(5)])
    res = pl.pallas_call(
        functools.partial(_proj_kernel, decode=decode),
        grid=(B, L // tm), in_specs=in_specs,
        out_specs=[tok(w) for _, w, _ in outs],
        out_shape=[jax.ShapeDtypeStruct((B, L, w), dt) for _, w, dt in outs],
        compiler_params=_params(("parallel", "parallel")),
        name="proj_decode" if decode else "proj_prompt",
    )(x, *weights, *tabs)
    return dict(zip([n for n, _, _ in outs], res))


def _dsa_kernel(qi_ref, wi_ref, ki_ref, qa_ref, ka_ref, va_ref, oa_ref, sel_ref, *, topk, n_buckets):
    tq, L = sel_ref.shape
    j = pl.program_id(1)
    rep = A_HEADS // A_KV_HEADS

    def body(ext):
        ki = ki_ref[0:ext, :]
        wi = wi_ref[...] * IDX_DIM ** -0.5
        score = jnp.zeros((tq, ext), f32)
        for hd in range(IDX_HEADS):
            s = lax.dot_general(qi_ref[:, hd * IDX_DIM:(hd + 1) * IDX_DIM], ki, _NT, preferred_element_type=f32)
            score = score + wi[:, hd:hd + 1] * jnp.maximum(s, 0.0)
        qpos = j * tq + lax.broadcasted_iota(jnp.int32, (tq, ext), 0)
        kpos = lax.broadcasted_iota(jnp.int32, (tq, ext), 1)
        key = _order_key(jnp.where(kpos <= qpos, score, -jnp.inf))
        sel_view = sel_ref.at[:, pl.ds(0, ext)]
        _topk_membership(key, topk, sel_view)
        sel = (sel_view[...] > 0.5) & (key > _NEG_INF_KEY)
        sel2 = jnp.concatenate([sel] * rep, axis=0)
        for g in range(A_KV_HEADS):
            kv = slice(g * A_HEAD_DIM, (g + 1) * A_HEAD_DIM)
            q = jnp.concatenate([qa_ref[:, (g * rep + r) * A_HEAD_DIM:(g * rep + r + 1) * A_HEAD_DIM]
                                 for r in range(rep)], axis=0)
            s = lax.dot_general(q, ka_ref[0:ext, kv], _NT, preferred_element_type=f32) * A_HEAD_DIM ** -0.5
            p = _softmax_rows(jnp.where(sel2, s, -jnp.inf))
            out = jnp.dot(p.astype(bf16), va_ref[0:ext, kv], preferred_element_type=f32).astype(bf16)
            for r in range(rep):
                hd = g * rep + r
                oa_ref[:, hd * A_HEAD_DIM:(hd + 1) * A_HEAD_DIM] = out[r * tq:(r + 1) * tq]

    _causal_buckets(j, tq, L, n_buckets, body)


def _dsa_prompt(pr, *, tq, topk, n_buckets):
    B, L, _ = pr['qa'].shape

    def qspec(width):
        return pl.BlockSpec((None, tq, width), lambda b, j: (b, j, 0))

    def kspec(width):
        return pl.BlockSpec((None, L, width), lambda b, j: (b, 0, 0))

    return pl.pallas_call(
        functools.partial(_dsa_kernel, topk=topk, n_buckets=n_buckets),
        grid=(B, L // tq),
        in_specs=[qspec(IDX_WIDTH), qspec(LANES), kspec(IDX_DIM), qspec(A_WIDTH), kspec(A_KV_WIDTH),
                  kspec(A_KV_WIDTH)],
        out_specs=qspec(A_WIDTH),
        out_shape=jax.ShapeDtypeStruct((B, L, A_WIDTH), bf16),
        scratch_shapes=[pltpu.VMEM((tq, L), f32)],
        compiler_params=_params(("parallel", "parallel")),
        name="dsa_prompt",
    )(pr['qi'], pr['wi'], pr['ki_b'], pr['qa'], pr['ka_b'], pr['va_b'])


def _mla_kernel(ql_ref, qr_ref, c_ref, kr_ref, ol_ref, *, n_buckets):
    tq = ql_ref.shape[0]
    L = c_ref.shape[0]
    j = pl.program_id(1)

    def body(ext):
        c = c_ref[0:ext, :]
        ql = jnp.concatenate([ql_ref[:, hd * KV_LORA:(hd + 1) * KV_LORA] for hd in range(C_HEADS)], axis=0)
        qr = jnp.concatenate([qr_ref[:, hd * C_ROPE:(hd + 1) * C_ROPE] for hd in range(C_HEADS)], axis=0)
        s = (lax.dot_general(ql, c, _NT, preferred_element_type=f32)
             + lax.dot_general(qr, kr_ref[0:ext, :], _NT, preferred_element_type=f32))
        row = lax.broadcasted_iota(jnp.int32, (C_HEADS * tq, ext), 0)
        qpos = j * tq + (row & (tq - 1))
        kpos = lax.broadcasted_iota(jnp.int32, (C_HEADS * tq, ext), 1)
        p = _softmax_rows(jnp.where(kpos <= qpos, s * (C_NOPE + C_ROPE) ** -0.5, -jnp.inf))
        out = jnp.dot(p.astype(bf16), c, preferred_element_type=f32).astype(bf16)
        for hd in range(C_HEADS):
            ol_ref[:, hd * KV_LORA:(hd + 1) * KV_LORA] = out[hd * tq:(hd + 1) * tq]

    _causal_buckets(j, tq, L, n_buckets, body)


def _mla_prompt(pr, *, tq, n_buckets):
    B, L, _ = pr['qlat'].shape
    assert tq & (tq - 1) == 0

    def qspec(width):
        return pl.BlockSpec((None, tq, width), lambda b, j: (b, j, 0))

    def kspec(width):
        return pl.BlockSpec((None, L, width), lambda b, j: (b, 0, 0))

    return pl.pallas_call(
        functools.partial(_mla_kernel, n_buckets=n_buckets),
        grid=(B, L // tq),
        in_specs=[qspec(C_HEADS * KV_LORA), qspec(C_HEADS * C_ROPE), kspec(KV_LORA), kspec(C_ROPE)],
        out_specs=qspec(C_HEADS * KV_LORA),
        out_shape=jax.ShapeDtypeStruct((B, L, C_HEADS * KV_LORA), bf16),
        compiler_params=_params(("parallel", "parallel")),
        name="mla_prompt",
    )(pr['qlat'], pr['qrope'], pr['ckv_b'], pr['kr_b'])


def _mix_kernel(x_ref, oa_ref, ob_ref, ol_ref, gain_ref, wg_ref, wa_ref, wb_ref, wc_ref, wuv_ref, wo_ref, x1_ref):
    x = x_ref[...]
    h = _rms(x, gain_ref[...]).astype(bf16)
    oc = jnp.concatenate(
        [jnp.dot(ol_ref[:, hd * KV_LORA:(hd + 1) * KV_LORA], wuv_ref[hd], preferred_element_type=f32)
         for hd in range(C_HEADS)], axis=1).astype(bf16)
    merged = None
    for k, (o, w_ref) in enumerate(((oa_ref[...], wa_ref), (ob_ref[...], wb_ref), (oc, wc_ref))):
        gate = jax.nn.sigmoid(jnp.dot(h, wg_ref[:, k * D_MODEL:(k + 1) * D_MODEL], preferred_element_type=f32))
        term = gate * jnp.dot(o, w_ref[...], preferred_element_type=f32)
        merged = term if merged is None else merged + term
    x1_ref[...] = x + jnp.dot(merged.astype(bf16), wo_ref[...], preferred_element_type=f32)


def _mixer(x, oa, ob, ol, lw, *, tm):
    B, L, _ = x.shape

    def tok(width):
        return pl.BlockSpec((None, tm, width), lambda b, i: (b, i, 0))

    weights = [lw['attn_norm'], lw['w_gate'], lw['w_br_a'], lw['w_br_b'], lw['w_br_c'], lw['c_w_uv'], lw['w_out']]
    return pl.pallas_call(
        _mix_kernel,
        grid=(B, L // tm),
        in_specs=[tok(D_MODEL), tok(A_WIDTH), tok(B_WIDTH), tok(C_HEADS * KV_LORA)]
        + [_const_spec(w.shape) for w in weights],
        out_specs=tok(D_MODEL),
        out_shape=jax.ShapeDtypeStruct((B, L, D_MODEL), f32),
        compiler_params=_params(("parallel", "parallel")),
        name="mixer",
    )(x, oa, ob, ol, *weights)


def _ffn_kernel(*refs, decode, final):
    if decode:
        x1_ref, gain_ref, wa_ref, wg_ref, cw_ref, cb_ref, wd_ref, fin_ref, h0_ref, h1_ref = refs[:10]
        outs = refs[10:]
    else:
        x1_ref, gain_ref, wa_ref, wg_ref, cw_ref, cb_ref, wd_ref, fin_ref = refs[:8]
        outs = refs[8:]
    x2_ref, hist_ref = outs[0], outs[1]
    tm = x1_ref.shape[0]
    x1 = x1_ref[...]
    h = _rms(x1, gain_ref[...]).astype(bf16)
    a = jnp.dot(h, wa_ref[...], preferred_element_type=f32)
    g = jnp.dot(h, wg_ref[...], preferred_element_type=f32)
    cw = cw_ref[...]
    if decode:
        hist_ref[...] = g
        conv = g * cw[2:3] + cb_ref[...] + h0_ref[...] * cw[0:1] + h1_ref[...] * cw[1:2]
    else:
        gbuf = outs[-1]

        @pl.when(pl.program_id(1) == 0)
        def _():
            gbuf[0:SUBLANES, :] = jnp.zeros((SUBLANES, D_FF), f32)

        gbuf[SUBLANES:SUBLANES + tm, :] = g
        conv = g * cw[2:3] + cb_ref[...]
        for k in range(CONV_W - 1):
            off = SUBLANES - (CONV_W - 1) + k
            conv = conv + gbuf[off:off + tm, :] * cw[k:k + 1]
        hist_ref[...] = g[tm - (CONV_W - 1):, :]
        gbuf[0:SUBLANES, :] = g[tm - SUBLANES:, :]
    y = jnp.dot((jax.nn.silu(conv) * a).astype(bf16), wd_ref[...], preferred_element_type=f32)
    x2 = x1 + y
    x2_ref[...] = x2
    if final:
        outs[2][...] = _rms(x2, fin_ref[...])


def _ffn(x1, lw, final_norm, *, tm, final, hist=None):
    B, L, _ = x1.shape
    decode = hist is not None

    def tok(width):
        return pl.BlockSpec((None, tm, width), lambda b, i: (b, i, 0))

    weights = [lw['ffn_norm'], lw['ffn_w_a'], lw['ffn_w_g'], lw['ffn_conv_w'], lw['ffn_conv_b'], lw['ffn_w_down'],
               final_norm]
    in_specs = [tok(D_MODEL)] + [_const_spec(w.shape) for w in weights]
    args = [x1, *weights]
    out_specs = [tok(D_MODEL)]
    out_shape = [jax.ShapeDtypeStruct((B, L, D_MODEL), f32)]
    scratch = []
    if decode:
        in_specs += [tok(D_FF), tok(D_FF)]
        args += list(hist)
        out_specs.append(tok(D_FF))
        out_shape.append(jax.ShapeDtypeStruct((B, L, D_FF), f32))
    else:
        out_specs.append(pl.BlockSpec((None, CONV_W - 1, D_FF), lambda b, i: (b, 0, 0)))
        out_shape.append(jax.ShapeDtypeStruct((B, CONV_W - 1, D_FF), f32))
        scratch.append(pltpu.VMEM((tm + SUBLANES, D_FF), f32))
    if final:
        out_specs.append(tok(D_MODEL))
        out_shape.append(jax.ShapeDtypeStruct((B, L, D_MODEL), f32))
    return pl.pallas_call(
        functools.partial(_ffn_kernel, decode=decode, final=final),
        grid=(B, L // tm), in_specs=in_specs, out_specs=out_specs, out_shape=out_shape,
        scratch_shapes=scratch,
        compiler_params=_params(("parallel", "arbitrary")),
        name="ffn_decode" if decode else "ffn_prompt",
    )(*args)


def _sweep1_kernel(pt_ref, qi_ref, wi_ref, ql_ref, qr_ref, kin_ref, cn_ref, krn_ref, *rest, pages, n_steps):
    idx_refs, lat_refs, kro_refs = rest[:pages], rest[pages:2 * pages], rest[2 * pages:3 * pages]
    sc_ref, ol_ref, m_ref, l_ref, acc_ref = rest[3 * pages:]
    s = pl.program_id(1)
    scale_c = (C_NOPE + C_ROPE) ** -0.5

    @pl.when(s == 0)
    def _():
        m_ref[...] = jnp.full(m_ref.shape, -jnp.inf, f32)
        l_ref[...] = jnp.zeros(l_ref.shape, f32)
        acc_ref[...] = jnp.zeros(acc_ref.shape, f32)

    def online(sm, values):
        m_old = m_ref[...]
        m_new = jnp.maximum(m_old, jnp.max(sm, axis=1, keepdims=True))
        alpha = jnp.exp(m_old - m_new)
        p = jnp.exp(sm - m_new)
        l_ref[...] = alpha * l_ref[...] + jnp.sum(p, axis=1, keepdims=True)
        acc_ref[...] = alpha * acc_ref[...] + values(p)
        m_ref[...] = m_new

    @pl.when(s < n_steps)
    def _():
        qi, wi = qi_ref[...], wi_ref[...]
        ql, qr = ql_ref[...], qr_ref[...]
        sc_parts, sm_parts = [], []
        for i in range(pages):
            si = jnp.dot(qi, idx_refs[i][...].astype(bf16), preferred_element_type=f32)
            sc_parts.append(jnp.sum(wi * jnp.maximum(si * IDX_DIM ** -0.5, 0.0), axis=0, keepdims=True))
            sm_parts.append((lax.dot_general(ql, lat_refs[i][...].astype(bf16), _NT, preferred_element_type=f32)
                             + jnp.dot(qr, kro_refs[i][...].astype(bf16), preferred_element_type=f32)) * scale_c)
        sc_ref[...] = jnp.concatenate(sc_parts, axis=1)

        def values(p):
            pb = p.astype(bf16)
            out = None
            for i in range(pages):
                t = jnp.dot(pb[:, i * PAGE_SIZE:(i + 1) * PAGE_SIZE], lat_refs[i][...].astype(bf16),
                            preferred_element_type=f32)
                out = t if out is None else out + t
            return out

        online(jnp.concatenate(sm_parts, axis=1), values)

    @pl.when(s == n_steps)
    def _():
        s_new = jnp.sum(qi_ref[...].astype(f32) * kin_ref[...], axis=1, keepdims=True)
        sc_new = jnp.sum(wi_ref[...] * jnp.maximum(s_new * IDX_DIM ** -0.5, 0.0), axis=0, keepdims=True)
        lane = lax.broadcasted_iota(jnp.int32, sc_ref.shape, 1)
        sc_ref[...] = jnp.where(lane == 0, sc_new, -jnp.inf)
        cn = cn_ref[...]
        krn = krn_ref[...]
        sm = (jnp.sum(ql_ref[...].astype(f32) * cn, axis=1, keepdims=True)
              + jnp.sum(qr_ref[...].astype(f32) * krn, axis=1, keepdims=True)) * scale_c
        online(sm, lambda p: p * cn)
        ol_ref[...] = acc_ref[...] / l_ref[...]


def _page_spec(cache, layer, i, pages, n_steps):
    return pl.BlockSpec(
        (None, None) + cache.shape[2:],
        lambda b, s, pt: (layer, pt[b, jnp.minimum(s, n_steps - 1) * pages + i], 0, 0))


def _per_seq_spec(a):
    nd = a.ndim - 1
    return pl.BlockSpec((None,) + a.shape[1:], lambda b, s, pt: (b,) + (0,) * nd)


def _sweep1(sr, caches, page_table, layer, *, pages):
    S, n_pages = page_table.shape
    n_steps = n_pages // pages
    width = pages * PAGE_SIZE
    in_specs = [_per_seq_spec(a) for a in sr]
    for cache in caches:
        in_specs += [_page_spec(cache, layer, i, pages, n_steps) for i in range(pages)]
    grid_spec = pltpu.PrefetchScalarGridSpec(
        num_scalar_prefetch=1, grid=(S, n_steps + 1), in_specs=in_specs,
        out_specs=[pl.BlockSpec((None, None, 1, width), lambda b, s, pt: (b, s, 0, 0)),
                   pl.BlockSpec((None, _ROW_PAD, KV_LORA), lambda b, s, pt: (b, 0, 0))],
        scratch_shapes=[pltpu.VMEM((_ROW_PAD, 1), f32), pltpu.VMEM((_ROW_PAD, 1), f32),
                        pltpu.VMEM((_ROW_PAD, KV_LORA), f32)])
    return pl.pallas_call(
        functools.partial(_sweep1_kernel, pages=pages, n_steps=n_steps),
        grid_spec=grid_spec,
        out_shape=[jax.ShapeDtypeStruct((S, n_steps + 1, 1, width), f32),
                   jax.ShapeDtypeStruct((S, _ROW_PAD, KV_LORA), f32)],
        compiler_params=_params(("parallel", "arbitrary")),
        name="decode_indexer_mla",
    )(page_table, *sr, *[c for cache in caches for c in [cache] * pages])


def _select_kernel(sc_ref, sel_ref, *, topk):
    _topk_membership(_order_key(sc_ref[...]), topk, sel_ref)


def _select(scores, *, topk):
    return pl.pallas_call(
        functools.partial(_select_kernel, topk=topk),
        out_shape=jax.ShapeDtypeStruct(scores.shape, f32),
        compiler_params=pltpu.CompilerParams(vmem_limit_bytes=VMEM_LIMIT),
        name="decode_select",
    )(scores)


def _sweep2_kernel(pt_ref, qa_ref, kn_ref, vn_ref, sel_ref, *rest, pages, n_steps):
    k_refs, v_refs = rest[:pages], rest[pages:2 * pages]
    oa_ref, m_ref, l_ref, acc_ref = rest[2 * pages:]
    s = pl.program_id(1)
    rep = A_HEADS // A_KV_HEADS
    row = lax.broadcasted_iota(jnp.int32, (_ROW_PAD, 1), 0)
    first_group = row < rep
    scale = A_HEAD_DIM ** -0.5

    @pl.when(s == 0)
    def _():
        m_ref[...] = jnp.full(m_ref.shape, -jnp.inf, f32)
        l_ref[...] = jnp.zeros(l_ref.shape, f32)
        acc_ref[...] = jnp.zeros(acc_ref.shape, f32)

    def online(sm, keep, values):
        m_old = m_ref[...]
        m_new = jnp.maximum(m_old, jnp.max(jnp.where(keep, sm, -jnp.inf), axis=1, keepdims=True))
        m_use = jnp.where(m_new == -jnp.inf, 0.0, m_new)
        alpha = jnp.exp(m_old - m_use)
        p = jnp.where(keep, jnp.exp(sm - m_use), 0.0)
        l_ref[...] = alpha * l_ref[...] + jnp.sum(p, axis=1, keepdims=True)
        acc_ref[...] = alpha * acc_ref[...] + values(p)
        m_ref[...] = m_new

    def head_rows(ref, g):
        return ref[pl.ds(g, PAGE_SIZE, stride=A_KV_HEADS), :].astype(bf16)

    @pl.when(s < n_steps)
    def _():
        q = qa_ref[...]
        parts = []
        for i in range(pages):
            s0 = lax.dot_general(q, head_rows(k_refs[i], 0), _NT, preferred_element_type=f32)
            s1 = lax.dot_general(q, head_rows(k_refs[i], 1), _NT, preferred_element_type=f32)
            parts.append(jnp.where(first_group, s0, s1) * scale)

        def values(p):
            pb = p.astype(bf16)
            out = None
            for i in range(pages):
                pi = pb[:, i * PAGE_SIZE:(i + 1) * PAGE_SIZE]
                t = jnp.where(first_group,
                              jnp.dot(pi, head_rows(v_refs[i], 0), preferred_element_type=f32),
                              jnp.dot(pi, head_rows(v_refs[i], 1), preferred_element_type=f32))
                out = t if out is None else out + t
            return out

        online(jnp.concatenate(parts, axis=1), sel_ref[...] > 0.5, values)

    @pl.when(s == n_steps)
    def _():
        q = qa_ref[...].astype(f32)
        kn = kn_ref[...]
        vn = vn_ref[...]
        k_row = jnp.where(first_group, kn[:, :A_HEAD_DIM], kn[:, A_HEAD_DIM:])
        v_row = jnp.where(first_group, vn[:, :A_HEAD_DIM], vn[:, A_HEAD_DIM:])
        sm = jnp.sum(q * k_row, axis=1, keepdims=True) * scale
        online(sm, sel_ref[:, 0:1] > 0.5, lambda p: p * v_row)
        oa_ref[...] = acc_ref[...] / l_ref[...]


def _sweep2(qa, kn, vn, sel, cache_k, cache_v, page_table, layer, *, pages):
    S, n_pages = page_table.shape
    n_steps = n_pages // pages
    width = pages * PAGE_SIZE
    in_specs = [_per_seq_spec(qa), _per_seq_spec(kn), _per_seq_spec(vn),
                pl.BlockSpec((None, None, 1, width), lambda b, s, pt: (b, s, 0, 0))]
    for cache in (cache_k, cache_v):
        in_specs += [_page_spec(cache, layer, i, pages, n_steps) for i in range(pages)]
    grid_spec = pltpu.PrefetchScalarGridSpec(
        num_scalar_prefetch=1, grid=(S, n_steps + 1), in_specs=in_specs,
        out_specs=pl.BlockSpec((None, _ROW_PAD, A_HEAD_DIM), lambda b, s, pt: (b, 0, 0)),
        scratch_shapes=[pltpu.VMEM((_ROW_PAD, 1), f32), pltpu.VMEM((_ROW_PAD, 1), f32),
                        pltpu.VMEM((_ROW_PAD, A_HEAD_DIM), f32)])
    return pl.pallas_call(
        functools.partial(_sweep2_kernel, pages=pages, n_steps=n_steps),
        grid_spec=grid_spec,
        out_shape=jax.ShapeDtypeStruct((S, _ROW_PAD, A_HEAD_DIM), f32),
        compiler_params=_params(("parallel", "arbitrary")),
        name="decode_dsa",
    )(page_table, qa, kn, vn, sel, *([cache_k] * pages), *([cache_v] * pages))


def _rope_tables(pos):
    def cs(half):
        inv = ROPE_THETA ** (-jnp.arange(half, dtype=f32) / half)
        ang = pos.astype(f32)[:, None] * inv[None, :]
        return jnp.cos(ang), jnp.sin(ang)

    c, s = cs(A_HEAD_DIM // 2)
    cos_a = jnp.concatenate([c, c], axis=1)
    sin_a = jnp.concatenate([-s, s], axis=1)
    c, s = cs(IDX_DIM // 2)
    z = jnp.zeros_like(s)
    cos_i = jnp.concatenate([c, c, c, c], axis=1)
    sin_lo = jnp.concatenate([-s, z, -s, z], axis=1)
    sin_hi = jnp.concatenate([z, s, z, s], axis=1)
    return cos_a, sin_a, cos_i, sin_lo, sin_hi


def _layer_weights(l, w_in, attn_norm, c_q_norm, c_w_uq, c_kv_norm, c_w_uk, c_w_uv, b_v_norm, b_w_s, b_bias,
                   w_br_a, w_br_b, w_br_c, w_out, ffn_norm, ffn_w_up, ffn_conv_w, ffn_conv_b, ffn_w_down):
    w = w_in[l]
    o_ki = A_WIDTH + 2 * A_KV_WIDTH + IDX_WIDTH
    o_wi = o_ki + IDX_DIM
    o_u = o_wi + IDX_HEADS
    o_cq = o_u + 2 * B_WIDTH
    o_kr = o_cq + Q_LORA + KV_LORA
    o_gl = o_kr + C_ROPE
    w_all = jnp.concatenate(
        [w[:, :o_ki], w[:, o_ki:o_wi], w[:, o_kr:o_gl],
         jnp.pad(w[:, o_wi:o_u], ((0, 0), (0, LANES - IDX_HEADS))), w[:, o_u:o_cq], w[:, o_cq:o_kr]], axis=1)
    uq = c_w_uq[l]
    return {
        'attn_norm': attn_norm[l][None], 'w_all': w_all.astype(bf16), 'w_gate': w[:, o_gl:].astype(bf16),
        'c_q_norm': c_q_norm[l][None], 'c_kv_norm': c_kv_norm[l][None], 'b_v_norm': b_v_norm[l][None],
        'c_w_uq': jnp.concatenate([uq[:, :, :C_NOPE].reshape(Q_LORA, C_HEADS * C_NOPE),
                                   uq[:, :, C_NOPE:].reshape(Q_LORA, C_HEADS * C_ROPE)], axis=1).astype(bf16),
        'c_w_uk': jnp.transpose(c_w_uk[l], (1, 2, 0)).astype(bf16),
        'c_w_uv': jnp.transpose(c_w_uv[l], (1, 0, 2)).astype(bf16),
        'b_w_s': b_w_s[l], 'b_bias_t': b_bias[l].T,
        'b_w_00': jnp.repeat(b_w_s[l][:, 0, 0], B_GROUP_DIM)[None], 'b_bias_0': jnp.repeat(b_bias[l][:, 0], B_GROUP_DIM)[None],
        'w_br_a': w_br_a[l].astype(bf16), 'w_br_b': w_br_b[l].astype(bf16), 'w_br_c': w_br_c[l].astype(bf16),
        'w_out': w_out[l].astype(bf16), 'ffn_norm': ffn_norm[l][None],
        'ffn_w_a': ffn_w_up[l][:, :D_FF].astype(bf16), 'ffn_w_g': ffn_w_up[l][:, D_FF:].astype(bf16),
        'ffn_conv_w': ffn_conv_w[l], 'ffn_conv_b': ffn_conv_b[l][None], 'ffn_w_down': ffn_w_down[l].astype(bf16),
    }


def _pad_rows(a):
    return jnp.pad(a, ((0, 0), (0, _ROW_PAD - a.shape[1]), (0, 0)))


TM_PROJ = 256
TM_MIX = 256
TM_FFN = 256
TQ_DSA = 256
TQ_MLA = 128
CAUSAL_BUCKETS = 8
SWEEP_PAGES = 32


def kernel(x_prompt, x_sample, cache_a_k, cache_a_v, cache_a_idxk, cache_c_latent, cache_c_krope, state_ffn_conv, page_table, attn_norm, w_in, c_q_norm, c_w_uq, c_kv_norm, c_w_uk, c_w_uv, b_v_norm, b_w_s, b_bias, w_br_a, w_br_b, w_br_c, w_out, ffn_norm, ffn_w_up, ffn_conv_w, ffn_conv_b, ffn_w_down, final_norm):
    bsz, seq, _ = x_prompt.shape
    dbsz, dseq, _ = x_sample.shape
    depth = w_in.shape[0]
    assert dseq == 1 and seq % CHUNK == 0
    n_pages = page_table.shape[1]
    past_len = n_pages * PAGE_SIZE
    assert past_len % CHUNK == 0
    topk_p = min(TOPK_MAX, seq // 4)
    topk_s = min(TOPK_MAX, (past_len + dseq) // 4)
    n_buckets = min(CAUSAL_BUCKETS, seq // max(TQ_DSA, TQ_MLA, topk_p))
    pages = min(SWEEP_PAGES, n_pages)
    tabs_p = _rope_tables(jnp.arange(seq, dtype=jnp.int32))
    tabs_s = _rope_tables(jnp.full((dbsz,), past_len, dtype=jnp.int32))
    fin = final_norm[None]
    cache_k2 = cache_a_k.reshape(cache_a_k.shape[:2] + (PAGE_SIZE * A_KV_HEADS, A_HEAD_DIM))
    cache_v2 = cache_a_v.reshape(cache_a_v.shape[:2] + (PAGE_SIZE * A_KV_HEADS, A_HEAD_DIM))
    cache_idx_t = jnp.swapaxes(cache_a_idxk, 2, 3)
    cache_kro_t = jnp.swapaxes(cache_c_krope, 2, 3)

    xp = x_prompt
    xs = x_sample.reshape(1, dbsz, D_MODEL)
    outs = {k: [] for k in ('pk', 'pv', 'pik', 'pcl', 'pcr', 'pfc', 'sk', 'sv', 'sik', 'scl', 'scr', 'sfc', 'sbv')}
    yp = ys = None
    for l in range(depth):
        lw = _layer_weights(l, w_in, attn_norm, c_q_norm, c_w_uq, c_kv_norm, c_w_uk, c_w_uv, b_v_norm, b_w_s,
                            b_bias, w_br_a, w_br_b, w_br_c, w_out, ffn_norm, ffn_w_up, ffn_conv_w, ffn_conv_b,
                            ffn_w_down)
        final = l == depth - 1

        pr = _project(xp, dict(lw, gate_w=lw['b_w_s'], gate_b=lw['b_bias_t']), tabs_p, tm=TM_PROJ, decode=False)
        oa = _dsa_prompt(pr, tq=TQ_DSA, topk=topk_p, n_buckets=n_buckets)
        ol = _mla_prompt(pr, tq=TQ_MLA, n_buckets=n_buckets)
        x1 = _mixer(xp, oa, pr['ob'], ol, lw, tm=TM_MIX)
        res = _ffn(x1, lw, fin, tm=TM_FFN, final=final)
        xp = res[0]
        if final:
            yp = res[2]
        outs['pk'].append(pr['ka'].reshape(bsz, seq, A_KV_HEADS, A_HEAD_DIM))
        outs['pv'].append(pr['va'].reshape(bsz, seq, A_KV_HEADS, A_HEAD_DIM))
        outs['pik'].append(pr['ki']); outs['pcl'].append(pr['ckv']); outs['pcr'].append(pr['kr'])
        outs['pfc'].append(res[1])

        sr = _project(xs, dict(lw, gate_w=lw['b_w_00'], gate_b=lw['b_bias_0']), tabs_s, tm=dbsz, decode=True)
        seqs = lambda a, r: a.reshape(dbsz, r, a.shape[-1] // r)
        sweep_in = (_pad_rows(seqs(sr['qi'], IDX_HEADS)),
                    _pad_rows(sr['wi'][0, :, :IDX_HEADS, None]),
                    _pad_rows(seqs(sr['qlat'], C_HEADS)), _pad_rows(seqs(sr['qrope'], C_HEADS)),
                    seqs(sr['ki'], 1), seqs(sr['ckv'], 1), seqs(sr['kr'], 1))
        scores, ol_s = _sweep1(sweep_in, (cache_idx_t, cache_c_latent, cache_kro_t), page_table, l, pages=pages)
        sel = _select(scores.reshape(dbsz, -1), topk=topk_s).reshape(scores.shape)
        oa_s = _sweep2(_pad_rows(seqs(sr['qa'], A_HEADS)), seqs(sr['ka'], 1), seqs(sr['va'], 1), sel,
                       cache_k2, cache_v2, page_table, l, pages=pages)
        oa_s = oa_s[:, :A_HEADS].reshape(1, dbsz, A_WIDTH).astype(bf16)
        ol_s = ol_s[:, :C_HEADS].reshape(1, dbsz, C_HEADS * KV_LORA).astype(bf16)
        x1s = _mixer(xs, oa_s, sr['ob'], ol_s, lw, tm=dbsz)
        hist = state_ffn_conv[l]
        res_s = _ffn(x1s, lw, fin, tm=dbsz, final=final,
                     hist=(hist[None, :, 0], hist[None, :, 1]))
        xs = res_s[0]
        if final:
            ys = res_s[2]
        outs['sk'].append(sr['ka'].reshape(dbsz, dseq, A_KV_HEADS, A_HEAD_DIM))
        outs['sv'].append(sr['va'].reshape(dbsz, dseq, A_KV_HEADS, A_HEAD_DIM))
        outs['sik'].append(sr['ki'].reshape(dbsz, dseq, IDX_DIM))
        outs['scl'].append(sr['ckv'].reshape(dbsz, dseq, KV_LORA))
        outs['scr'].append(sr['kr'].reshape(dbsz, dseq, C_ROPE))
        outs['sfc'].append(jnp.stack([hist[:, 1], res_s[1][0]], axis=1))
        outs['sbv'].append(sr['vn'].reshape(dbsz, dseq, B_WIDTH))

    st = {k: jnp.stack(v) for k, v in outs.items()}
    return (yp, ys.reshape(dbsz, dseq, D_MODEL), st['pk'], st['pv'], st['pik'], st['pcl'], st['pcr'], st['pfc'],
            st['sk'], st['sv'], st['sik'], st['scl'], st['scr'], st['sfc'], st['sbv'])
```

```python
import functools

import jax
import jax.numpy as jnp
from jax import lax
from jax.experimental import pallas as pl
from jax.experimental.pallas import tpu as pltpu

D_MODEL = 1024
PAGE_SIZE = 128
A_HEADS = 4
A_KV_HEADS = 2
A_HEAD_DIM = 128
A_WIDTH = A_HEADS * A_HEAD_DIM
A_KV_WIDTH = A_KV_HEADS * A_HEAD_DIM
IDX_HEADS = 8
IDX_DIM = 64
IDX_WIDTH = IDX_HEADS * IDX_DIM
TOPK_MAX = 256
CHUNK = 128
B_GROUPS = 4
B_GROUP_DIM = 128
B_WIDTH = B_GROUPS * B_GROUP_DIM
C_HEADS = 4
Q_LORA = 256
KV_LORA = 256
C_NOPE = 128
C_ROPE = 64
C_VDIM = 128
C_WIDTH = C_HEADS * C_VDIM
N_BRANCH = 3
D_FF = 2816
CONV_W = 3
ROPE_THETA = 10000.0
EPS = 1e-6

LANES = 128
SUBLANES = 8
VMEM_LIMIT = 56 * 1024 * 1024

_O_QA = 0
_O_KA = _O_QA + A_WIDTH
_O_VA = _O_KA + A_KV_WIDTH
_O_QI = _O_VA + A_KV_WIDTH
_O_KIKR = _O_QI + IDX_WIDTH
_O_WI = _O_KIKR + LANES
_O_U = _O_WI + LANES
_O_V = _O_U + B_WIDTH
_O_CQ = _O_V + B_WIDTH
_O_CKV = _O_CQ + Q_LORA
_W_ALL = _O_CKV + KV_LORA

_NT = (((1,), (1,)), ((), ()))
_INT_MIN = -2147483648
_NEG_INF_KEY = -2139095041
_ROW_PAD = 16
_SEARCH_GROUP_ROWS = 64

f32 = jnp.float32
bf16 = jnp.bfloat16


def _rms(x, g):
    return x * lax.rsqrt(jnp.mean(x * x, axis=-1, keepdims=True) + EPS) * g


def _rope_full(x, cos, sin):
    parts = []
    for k in range(x.shape[1] // LANES):
        xs = x[:, k * LANES:(k + 1) * LANES]
        parts.append(xs * cos + pltpu.roll(xs, LANES // 2, 1) * sin)
    return parts[0] if len(parts) == 1 else jnp.concatenate(parts, axis=1)


def _rope_half(x, cos, sin_lo, sin_hi):
    parts = []
    for k in range(x.shape[1] // LANES):
        xs = x[:, k * LANES:(k + 1) * LANES]
        parts.append(xs * cos + pltpu.roll(xs, LANES - IDX_DIM // 2, 1) * sin_lo
                     + pltpu.roll(xs, IDX_DIM // 2, 1) * sin_hi)
    return parts[0] if len(parts) == 1 else jnp.concatenate(parts, axis=1)


def _order_key(score):
    bits = lax.bitcast_convert_type(score, jnp.int32)
    return bits ^ ((bits >> 31) & jnp.int32(0x7FFFFFFF))


def _count(mask):
    return jnp.sum(jnp.where(mask, 1.0, 0.0), axis=1, keepdims=True)


def _kth_largest_key(key, k):
    rows = key.shape[0]
    group = min(rows, _SEARCH_GROUP_ROWS)
    parts = [key[r:r + group] for r in range(0, rows, group)]

    def body(i, tbs):
        bit = jnp.left_shift(jnp.int32(1), 31 - i)
        out = []
        for part, tb in zip(parts, tbs):
            cand_b = tb | bit
            cnt = _count(part >= (cand_b ^ jnp.int32(_INT_MIN)))
            out.append(jnp.where(cnt >= k, cand_b, tb))
        return tuple(out)

    tbs = lax.fori_loop(0, 32, body, tuple(jnp.zeros((group, 1), jnp.int32) for _ in parts))
    tb = tbs[0] if len(tbs) == 1 else jnp.concatenate(tbs, axis=0)
    return tb ^ jnp.int32(_INT_MIN)


def _topk_membership(key, k, sel_ref):
    rows, n = key.shape
    thr = _kth_largest_key(key, k)
    gt = key > thr
    eq = key == thr
    need = k - _count(gt)
    sel_ref[...] = jnp.where(gt | eq, 1.0, 0.0)
    tie = (thr > _NEG_INF_KEY) & (_count(eq) > need)

    @pl.when(jnp.max(jnp.where(tie, 1.0, 0.0)) > 0.0)
    def _():
        r = lax.broadcasted_iota(jnp.int32, (LANES, LANES), 0)
        c = lax.broadcasted_iota(jnp.int32, (LANES, LANES), 1)
        upper = jnp.where(r < c, 1.0, 0.0).astype(bf16)
        run = jnp.zeros((rows, 1), f32)
        for j in range(n // LANES):
            sl = slice(j * LANES, (j + 1) * LANES)
            eq_j = jnp.where(eq[:, sl], 1.0, 0.0)
            before = jnp.dot(eq_j.astype(bf16), upper, preferred_element_type=f32) + run
            keep = gt[:, sl] | (eq[:, sl] & (before < need))
            sel_ref[:, sl] = jnp.where(keep, 1.0, 0.0)
            run = run + jnp.sum(eq_j, axis=1, keepdims=True)


def _softmax_rows(s):
    p = jnp.exp(s - jnp.max(s, axis=1, keepdims=True))
    return p / jnp.sum(p, axis=1, keepdims=True)


def _const_spec(shape):
    nd = len(shape)
    return pl.BlockSpec(shape, lambda *_: (0,) * nd, pipeline_mode=pl.Buffered(1))


def _params(sem):
    return pltpu.CompilerParams(dimension_semantics=sem, vmem_limit_bytes=VMEM_LIMIT)


def _causal_buckets(j, tq, L, n_buckets, body):
    step = L // n_buckets
    per = step // tq
    for bk in range(n_buckets):
        @pl.when((j >= bk * per) & (j < (bk + 1) * per))
        def _(bk=bk):
            body((bk + 1) * step)


_PROJ_OUTS = (('qa', A_WIDTH, bf16), ('ka', A_KV_WIDTH, f32), ('va', A_KV_WIDTH, f32), ('qi', IDX_WIDTH, bf16),
              ('ki', IDX_DIM, f32), ('kr', C_ROPE, f32), ('wi', LANES, f32), ('ob', B_WIDTH, bf16),
              ('ckv', KV_LORA, f32), ('qlat', C_HEADS * KV_LORA, bf16), ('qrope', C_HEADS * C_ROPE, bf16))
_PROJ_OUTS_PROMPT = (('ka_b', A_KV_WIDTH, bf16), ('va_b', A_KV_WIDTH, bf16), ('ki_b', IDX_DIM, bf16),
                     ('ckv_b', KV_LORA, bf16), ('kr_b', C_ROPE, bf16))
_PROJ_OUTS_DECODE = (('vn', B_WIDTH, f32),)


def _proj_kernel(x_ref, gain_ref, w_ref, cqn_ref, wuq_ref, wuk_ref, ckvn_ref, bvn_ref, ws_ref, bias_ref,
                 cosa_ref, sina_ref, cosi_ref, silo_ref, sihi_ref, *out_refs, decode):
    names = [n for n, _, _ in _PROJ_OUTS + (_PROJ_OUTS_DECODE if decode else _PROJ_OUTS_PROMPT)]
    o = dict(zip(names, out_refs))
    tm = x_ref.shape[0]
    h = _rms(x_ref[...], gain_ref[...]).astype(bf16)
    z = jnp.dot(h, w_ref[...], preferred_element_type=f32)
    cosa, sina = cosa_ref[...], sina_ref[...]
    cosi, silo, sihi = cosi_ref[...], silo_ref[...], sihi_ref[...]

    o['qa'][...] = _rope_full(z[:, _O_QA:_O_KA], cosa, sina).astype(bf16)
    ka = _rope_full(z[:, _O_KA:_O_VA], cosa, sina)
    va = z[:, _O_VA:_O_QI]
    o['ka'][...] = ka
    o['va'][...] = va
    o['qi'][...] = _rope_half(z[:, _O_QI:_O_KIKR], cosi, silo, sihi).astype(bf16)
    kikr = _rope_half(z[:, _O_KIKR:_O_WI], cosi, silo, sihi)
    o['ki'][...] = kikr[:, :IDX_DIM]
    o['kr'][...] = kikr[:, IDX_DIM:]
    o['wi'][...] = z[:, _O_WI:_O_U] * IDX_HEADS ** -0.5
    ckv = _rms(z[:, _O_CKV:_W_ALL], ckvn_ref[...])
    o['ckv'][...] = ckv
    if not decode:
        o['ka_b'][...] = ka.astype(bf16)
        o['va_b'][...] = va.astype(bf16)
        kikr_b = kikr.astype(bf16)
        o['ki_b'][...] = kikr_b[:, :IDX_DIM]
        o['kr_b'][...] = kikr_b[:, IDX_DIM:]
        o['ckv_b'][...] = ckv.astype(bf16)

    u = jax.nn.gelu(z[:, _O_U:_O_V])
    v = _rms(jax.nn.gelu(z[:, _O_V:_O_CQ]), bvn_ref[...])
    if decode:
        o['vn'][...] = v
        o['ob'][...] = (u * (ws_ref[...] * v + bias_ref[...])).astype(bf16)
    else:
        r = lax.broadcasted_iota(jnp.int32, (CHUNK, CHUNK), 0)
        c = lax.broadcasted_iota(jnp.int32, (CHUNK, CHUNK), 1)
        vb = v.astype(bf16)
        bias = bias_ref[...]
        for g in range(B_GROUPS):
            w_g = jnp.where(r >= c, ws_ref[g], 0.0).astype(bf16)
            cols = slice(g * B_GROUP_DIM, (g + 1) * B_GROUP_DIM)
            for k in range(tm // CHUNK):
                rows = slice(k * CHUNK, (k + 1) * CHUNK)
                mixed = jnp.dot(w_g, vb[rows, cols], preferred_element_type=f32) + bias[:, g:g + 1]
                o['ob'][rows, cols] = (u[rows, cols] * mixed).astype(bf16)

    cq = _rms(z[:, _O_CQ:_O_CKV], cqn_ref[...]).astype(bf16)
    qc = jnp.dot(cq, wuq_ref[...], preferred_element_type=f32)
    for hd in range(C_HEADS):
        q_nope = qc[:, hd * C_NOPE:(hd + 1) * C_NOPE].astype(bf16)
        o['qlat'][:, hd * KV_LORA:(hd + 1) * KV_LORA] = jnp.dot(
            q_nope, wuk_ref[hd], preferred_element_type=f32).astype(bf16)
    o['qrope'][...] = _rope_half(qc[:, C_HEADS * C_NOPE:], cosi, silo, sihi).astype(bf16)


def _project(x, lw, tabs, *, tm, decode):
    B, L, _ = x.shape
    outs = _PROJ_OUTS + (_PROJ_OUTS_DECODE if decode else _PROJ_OUTS_PROMPT)

    def tok(width):
        return pl.BlockSpec((None, tm, width), lambda b, i: (b, i, 0))

    weights = [lw['attn_norm'], lw['w_all'], lw['c_q_norm'], lw['c_w_uq'], lw['c_w_uk'], lw['c_kv_norm'],
               lw['b_v_norm'], lw['gate_w'], lw['gate_b']]
    in_specs = ([tok(D_MODEL)] + [_const_spec(w.shape) for w in weights]
                + [pl.BlockSpec((tm, LANES), lambda b, i: (i, 0)) for _ in range(5)])
    res = pl.pallas_call(
        functools.partial(_proj_kernel, decode=decode),
        grid=(B, L // tm), in_specs=in_specs,
        out_specs=[tok(w) for _, w, _ in outs],
        out_shape=[jax.ShapeDtypeStruct((B, L, w), dt) for _, w, dt in outs],
        compiler_params=_params(("parallel", "parallel")),
        name="proj_decode" if decode else "proj_prompt",
    )(x, *weights, *tabs)
    return dict(zip([n for n, _, _ in outs], res))


def _dsa_kernel(qi_ref, wi_ref, ki_ref, qa_ref, ka_ref, va_ref, oa_ref, sel_ref, *, topk, first_block):
    tq, ext = sel_ref.shape
    j = first_block + pl.program_id(1)
    rep = A_HEADS // A_KV_HEADS
    ki = ki_ref[...]
    wi = wi_ref[...] * IDX_DIM ** -0.5
    score = jnp.zeros((tq, ext), f32)
    for hd in range(IDX_HEADS):
        s = lax.dot_general(qi_ref[:, hd * IDX_DIM:(hd + 1) * IDX_DIM], ki, _NT, preferred_element_type=f32)
        score = score + wi[:, hd:hd + 1] * jnp.maximum(s, 0.0)
    qpos = j * tq + lax.broadcasted_iota(jnp.int32, (tq, ext), 0)
    kpos = lax.broadcasted_iota(jnp.int32, (tq, ext), 1)
    key = _order_key(jnp.where(kpos <= qpos, score, -jnp.inf))
    _topk_membership(key, topk, sel_ref)
    sel = (sel_ref[...] > 0.5) & (key > _NEG_INF_KEY)
    sel2 = jnp.concatenate([sel] * rep, axis=0)
    for g in range(A_KV_HEADS):
        kv = slice(g * A_HEAD_DIM, (g + 1) * A_HEAD_DIM)
        q = jnp.concatenate([qa_ref[:, (g * rep + r) * A_HEAD_DIM:(g * rep + r + 1) * A_HEAD_DIM]
                             for r in range(rep)], axis=0)
        s = lax.dot_general(q, ka_ref[:, kv], _NT, preferred_element_type=f32) * A_HEAD_DIM ** -0.5
        p = _softmax_rows(jnp.where(sel2, s, -jnp.inf))
        out = jnp.dot(p.astype(bf16), va_ref[:, kv], preferred_element_type=f32).astype(bf16)
        for r in range(rep):
            hd = g * rep + r
            oa_ref[:, hd * A_HEAD_DIM:(hd + 1) * A_HEAD_DIM] = out[r * tq:(r + 1) * tq]


def _dsa_prompt(pr, *, tq, topk, n_buckets):
    B, L, _ = pr['qa'].shape
    step = L // n_buckets
    per = step // tq
    pieces = []
    for bk in range(n_buckets):
        ext = (bk + 1) * step

        def qspec(width, bk=bk):
            return pl.BlockSpec((None, tq, width), lambda b, j: (b, bk * per + j, 0))

        def kspec(width, ext=ext):
            return pl.BlockSpec((None, ext, width), lambda b, j: (b, 0, 0))

        pieces.append(pl.pallas_call(
            functools.partial(_dsa_kernel, topk=topk, first_block=bk * per),
            grid=(B, per),
            in_specs=[qspec(IDX_WIDTH), qspec(LANES), kspec(IDX_DIM), qspec(A_WIDTH), kspec(A_KV_WIDTH),
                      kspec(A_KV_WIDTH)],
            out_specs=pl.BlockSpec((None, tq, A_WIDTH), lambda b, j: (b, j, 0)),
            out_shape=jax.ShapeDtypeStruct((B, step, A_WIDTH), bf16),
            scratch_shapes=[pltpu.VMEM((tq, ext), f32)],
            compiler_params=_params(("parallel", "parallel")),
            name=f"dsa_prompt_{ext}",
        )(pr['qi'], pr['wi'], pr['ki_b'], pr['qa'], pr['ka_b'], pr['va_b']))
    return jnp.concatenate(pieces, axis=1)


def _mla_kernel(ql_ref, qr_ref, c_ref, kr_ref, ol_ref, *, n_buckets):
    tq = ql_ref.shape[0]
    L = c_ref.shape[0]
    j = pl.program_id(1)

    def body(ext):
        c = c_ref[0:ext, :]
        ql = jnp.concatenate([ql_ref[:, hd * KV_LORA:(hd + 1) * KV_LORA] for hd in range(C_HEADS)], axis=0)
        qr = jnp.concatenate([qr_ref[:, hd * C_ROPE:(hd + 1) * C_ROPE] for hd in range(C_HEADS)], axis=0)
        s = (lax.dot_general(ql, c, _NT, preferred_element_type=f32)
             + lax.dot_general(qr, kr_ref[0:ext, :], _NT, preferred_element_type=f32))
        row = lax.broadcasted_iota(jnp.int32, (C_HEADS * tq, ext), 0)
        qpos = j * tq + (row & (tq - 1))
        kpos = lax.broadcasted_iota(jnp.int32, (C_HEADS * tq, ext), 1)
        p = _softmax_rows(jnp.where(kpos <= qpos, s * (C_NOPE + C_ROPE) ** -0.5, -jnp.inf))
        out = jnp.dot(p.astype(bf16), c, preferred_element_type=f32).astype(bf16)
        for hd in range(C_HEADS):
            ol_ref[:, hd * KV_LORA:(hd + 1) * KV_LORA] = out[hd * tq:(hd + 1) * tq]

    _causal_buckets(j, tq, L, n_buckets, body)


def _mla_prompt(pr, *, tq, n_buckets):
    B, L, _ = pr['qlat'].shape
    assert tq & (tq - 1) == 0

    def qspec(width):
        return pl.BlockSpec((None, tq, width), lambda b, j: (b, j, 0))

    def kspec(width):
        return pl.BlockSpec((None, L, width), lambda b, j: (b, 0, 0))

    return pl.pallas_call(
        functools.partial(_mla_kernel, n_buckets=n_buckets),
        grid=(B, L // tq),
        in_specs=[qspec(C_HEADS * KV_LORA), qspec(C_HEADS * C_ROPE), kspec(KV_LORA), kspec(C_ROPE)],
        out_specs=qspec(C_HEADS * KV_LORA),
        out_shape=jax.ShapeDtypeStruct((B, L, C_HEADS * KV_LORA), bf16),
        compiler_params=_params(("parallel", "parallel")),
        name="mla_prompt",
    )(pr['qlat'], pr['qrope'], pr['ckv_b'], pr['kr_b'])


def _mix_kernel(x_ref, oa_ref, ob_ref, ol_ref, gain_ref, wg_ref, wa_ref, wb_ref, wc_ref, wuv_ref, wo_ref, x1_ref):
    x = x_ref[...]
    h = _rms(x, gain_ref[...]).astype(bf16)
    oc = jnp.concatenate(
        [jnp.dot(ol_ref[:, hd * KV_LORA:(hd + 1) * KV_LORA], wuv_ref[hd], preferred_element_type=f32)
         for hd in range(C_HEADS)], axis=1).astype(bf16)
    merged = None
    for k, (o, w_ref) in enumerate(((oa_ref[...], wa_ref), (ob_ref[...], wb_ref), (oc, wc_ref))):
        gate = jax.nn.sigmoid(jnp.dot(h, wg_ref[:, k * D_MODEL:(k + 1) * D_MODEL], preferred_element_type=f32))
        term = gate * jnp.dot(o, w_ref[...], preferred_element_type=f32)
        merged = term if merged is None else merged + term
    x1_ref[...] = x + jnp.dot(merged.astype(bf16), wo_ref[...], preferred_element_type=f32)


def _mixer(x, oa, ob, ol, lw, *, tm):
    B, L, _ = x.shape

    def tok(width):
        return pl.BlockSpec((None, tm, width), lambda b, i: (b, i, 0))

    weights = [lw['attn_norm'], lw['w_gate'], lw['w_br_a'], lw['w_br_b'], lw['w_br_c'], lw['c_w_uv'], lw['w_out']]
    return pl.pallas_call(
        _mix_kernel,
        grid=(B, L // tm),
        in_specs=[tok(D_MODEL), tok(A_WIDTH), tok(B_WIDTH), tok(C_HEADS * KV_LORA)]
        + [_const_spec(w.shape) for w in weights],
        out_specs=tok(D_MODEL),
        out_shape=jax.ShapeDtypeStruct((B, L, D_MODEL), f32),
        compiler_params=_params(("parallel", "parallel")),
        name="mixer",
    )(x, oa, ob, ol, *weights)


def _ffn_kernel(*refs, decode, final):
    if decode:
        x1_ref, gain_ref, wa_ref, wg_ref, cw_ref, cb_ref, wd_ref, fin_ref, h0_ref, h1_ref = refs[:10]
        outs = refs[10:]
    else:
        x1_ref, gain_ref, wa_ref, wg_ref, cw_ref, cb_ref, wd_ref, fin_ref = refs[:8]
        outs = refs[8:]
    x2_ref, hist_ref = outs[0], outs[1]
    tm = x1_ref.shape[0]
    x1 = x1_ref[...]
    h = _rms(x1, gain_ref[...]).astype(bf16)
    a = jnp.dot(h, wa_ref[...], preferred_element_type=f32)
    g = jnp.dot(h, wg_ref[...], preferred_element_type=f32)
    cw = cw_ref[...]
    if decode:
        hist_ref[...] = g
        conv = g * cw[2:3] + cb_ref[...] + h0_ref[...] * cw[0:1] + h1_ref[...] * cw[1:2]
    else:
        gbuf = outs[-1]

        @pl.when(pl.program_id(1) == 0)
        def _():
            gbuf[0:SUBLANES, :] = jnp.zeros((SUBLANES, D_FF), f32)

        gbuf[SUBLANES:SUBLANES + tm, :] = g
        conv = g * cw[2:3] + cb_ref[...]
        for k in range(CONV_W - 1):
            off = SUBLANES - (CONV_W - 1) + k
            conv = conv + gbuf[off:off + tm, :] * cw[k:k + 1]
        hist_ref[...] = g[tm - (CONV_W - 1):, :]
        gbuf[0:SUBLANES, :] = g[tm - SUBLANES:, :]
    y = jnp.dot((jax.nn.silu(conv) * a).astype(bf16), wd_ref[...], preferred_element_type=f32)
    x2 = x1 + y
    x2_ref[...] = x2
    if final:
        outs[2][...] = _rms(x2, fin_ref[...])


def _ffn(x1, lw, final_norm, *, tm, final, hist=None):
    B, L, _ = x1.shape
    decode = hist is not None

    def tok(width):
        return pl.BlockSpec((None, tm, width), lambda b, i: (b, i, 0))

    weights = [lw['ffn_norm'], lw['ffn_w_a'], lw['ffn_w_g'], lw['ffn_conv_w'], lw['ffn_conv_b'], lw['ffn_w_down'],
               final_norm]
    in_specs = [tok(D_MODEL)] + [_const_spec(w.shape) for w in weights]
    args = [x1, *weights]
    out_specs = [tok(D_MODEL)]
    out_shape = [jax.ShapeDtypeStruct((B, L, D_MODEL), f32)]
    scratch = []
    if decode:
        in_specs += [tok(D_FF), tok(D_FF)]
        args += list(hist)
        out_specs.append(tok(D_FF))
        out_shape.append(jax.ShapeDtypeStruct((B, L, D_FF), f32))
    else:
        out_specs.append(pl.BlockSpec((None, CONV_W - 1, D_FF), lambda b, i: (b, 0, 0)))
        out_shape.append(jax.ShapeDtypeStruct((B, CONV_W - 1, D_FF), f32))
        scratch.append(pltpu.VMEM((tm + SUBLANES, D_FF), f32))
    if final:
        out_specs.append(tok(D_MODEL))
        out_shape.append(jax.ShapeDtypeStruct((B, L, D_MODEL), f32))
    return pl.pallas_call(
        functools.partial(_ffn_kernel, decode=decode, final=final),
        grid=(B, L // tm), in_specs=in_specs, out_specs=out_specs, out_shape=out_shape,
        scratch_shapes=scratch,
        compiler_params=_params(("parallel", "arbitrary")),
        name="ffn_decode" if decode else "ffn_prompt",
    )(*args)


def _sweep1_kernel(pt_ref, qi_ref, wi_ref, ql_ref, qr_ref, kin_ref, cn_ref, krn_ref, *rest, pages, n_steps):
    idx_refs, lat_refs, kro_refs = rest[:pages], rest[pages:2 * pages], rest[2 * pages:3 * pages]
    sc_ref, ol_ref, m_ref, l_ref, acc_ref = rest[3 * pages:]
    s = pl.program_id(1)
    scale_c = (C_NOPE + C_ROPE) ** -0.5

    @pl.when(s == 0)
    def _():
        m_ref[...] = jnp.full(m_ref.shape, -jnp.inf, f32)
        l_ref[...] = jnp.zeros(l_ref.shape, f32)
        acc_ref[...] = jnp.zeros(acc_ref.shape, f32)

    def online(sm, values):
        m_old = m_ref[...]
        m_new = jnp.maximum(m_old, jnp.max(sm, axis=1, keepdims=True))
        alpha = jnp.exp(m_old - m_new)
        p = jnp.exp(sm - m_new)
        l_ref[...] = alpha * l_ref[...] + jnp.sum(p, axis=1, keepdims=True)
        acc_ref[...] = alpha * acc_ref[...] + values(p)
        m_ref[...] = m_new

    @pl.when(s < n_steps)
    def _():
        qi, wi = qi_ref[...], wi_ref[...]
        ql, qr = ql_ref[...], qr_ref[...]
        sc_parts, sm_parts = [], []
        for i in range(pages):
            si = jnp.dot(qi, idx_refs[i][...].astype(bf16), preferred_element_type=f32)
            sc_parts.append(jnp.sum(wi * jnp.maximum(si * IDX_DIM ** -0.5, 0.0), axis=0, keepdims=True))
            sm_parts.append((lax.dot_general(ql, lat_refs[i][...].astype(bf16), _NT, preferred_element_type=f32)
                             + jnp.dot(qr, kro_refs[i][...].astype(bf16), preferred_element_type=f32)) * scale_c)
        sc_ref[...] = jnp.concatenate(sc_parts, axis=1)

        def values(p):
            pb = p.astype(bf16)
            out = None
            for i in range(pages):
                t = jnp.dot(pb[:, i * PAGE_SIZE:(i + 1) * PAGE_SIZE], lat_refs[i][...].astype(bf16),
                            preferred_element_type=f32)
                out = t if out is None else out + t
            return out

        online(jnp.concatenate(sm_parts, axis=1), values)

    @pl.when(s == n_steps)
    def _():
        s_new = jnp.sum(qi_ref[...].astype(f32) * kin_ref[...], axis=1, keepdims=True)
        sc_new = jnp.sum(wi_ref[...] * jnp.maximum(s_new * IDX_DIM ** -0.5, 0.0), axis=0, keepdims=True)
        lane = lax.broadcasted_iota(jnp.int32, sc_ref.shape, 1)
        sc_ref[...] = jnp.where(lane == 0, sc_new, -jnp.inf)
        cn = cn_ref[...]
        krn = krn_ref[...]
        sm = (jnp.sum(ql_ref[...].astype(f32) * cn, axis=1, keepdims=True)
              + jnp.sum(qr_ref[...].astype(f32) * krn, axis=1, keepdims=True)) * scale_c
        online(sm, lambda p: p * cn)
        ol_ref[...] = acc_ref[...] / l_ref[...]


def _page_spec(cache, layer, i, pages, n_steps):
    return pl.BlockSpec(
        (None, None) + cache.shape[2:],
        lambda b, s, pt: (layer, pt[b, jnp.minimum(s, n_steps - 1) * pages + i], 0, 0))


def _per_seq_spec(a):
    nd = a.ndim - 1
    return pl.BlockSpec((None,) + a.shape[1:], lambda b, s, pt: (b,) + (0,) * nd)


def _sweep1(sr, caches, page_table, layer, *, pages):
    S, n_pages = page_table.shape
    n_steps = n_pages // pages
    width = pages * PAGE_SIZE
    in_specs = [_per_seq_spec(a) for a in sr]
    for cache in caches:
        in_specs += [_page_spec(cache, layer, i, pages, n_steps) for i in range(pages)]
    grid_spec = pltpu.PrefetchScalarGridSpec(
        num_scalar_prefetch=1, grid=(S, n_steps + 1), in_specs=in_specs,
        out_specs=[pl.BlockSpec((None, None, 1, width), lambda b, s, pt: (b, s, 0, 0)),
                   pl.BlockSpec((None, _ROW_PAD, KV_LORA), lambda b, s, pt: (b, 0, 0))],
        scratch_shapes=[pltpu.VMEM((_ROW_PAD, 1), f32), pltpu.VMEM((_ROW_PAD, 1), f32),
                        pltpu.VMEM((_ROW_PAD, KV_LORA), f32)])
    return pl.pallas_call(
        functools.partial(_sweep1_kernel, pages=pages, n_steps=n_steps),
        grid_spec=grid_spec,
        out_shape=[jax.ShapeDtypeStruct((S, n_steps + 1, 1, width), f32),
                   jax.ShapeDtypeStruct((S, _ROW_PAD, KV_LORA), f32)],
        compiler_params=_params(("parallel", "arbitrary")),
        name="decode_indexer_mla",
    )(page_table, *sr, *[c for cache in caches for c in [cache] * pages])


def _select_kernel(sc_ref, sel_ref, *, topk):
    _topk_membership(_order_key(sc_ref[...]), topk, sel_ref)


def _select(scores, *, topk):
    return pl.pallas_call(
        functools.partial(_select_kernel, topk=topk),
        out_shape=jax.ShapeDtypeStruct(scores.shape, f32),
        compiler_params=pltpu.CompilerParams(vmem_limit_bytes=VMEM_LIMIT),
        name="decode_select",
    )(scores)


def _sweep2_kernel(pt_ref, qa_ref, kn_ref, vn_ref, sel_ref, *rest, pages, n_steps):
    k_refs, v_refs = rest[:pages], rest[pages:2 * pages]
    oa_ref, m_ref, l_ref, acc_ref = rest[2 * pages:]
    s = pl.program_id(1)
    rep = A_HEADS // A_KV_HEADS
    row = lax.broadcasted_iota(jnp.int32, (_ROW_PAD, 1), 0)
    first_group = row < rep
    scale = A_HEAD_DIM ** -0.5

    @pl.when(s == 0)
    def _():
        m_ref[...] = jnp.full(m_ref.shape, -jnp.inf, f32)
        l_ref[...] = jnp.zeros(l_ref.shape, f32)
        acc_ref[...] = jnp.zeros(acc_ref.shape, f32)

    def online(sm, keep, values):
        m_old = m_ref[...]
        m_new = jnp.maximum(m_old, jnp.max(jnp.where(keep, sm, -jnp.inf), axis=1, keepdims=True))
        m_use = jnp.where(m_new == -jnp.inf, 0.0, m_new)
        alpha = jnp.exp(m_old - m_use)
        p = jnp.where(keep, jnp.exp(sm - m_use), 0.0)
        l_ref[...] = alpha * l_ref[...] + jnp.sum(p, axis=1, keepdims=True)
        acc_ref[...] = alpha * acc_ref[...] + values(p)
        m_ref[...] = m_new

    def head_rows(ref, g):
        return ref[pl.ds(g, PAGE_SIZE, stride=A_KV_HEADS), :].astype(bf16)

    @pl.when(s < n_steps)
    def _():
        q = qa_ref[...]
        parts = []
        for i in range(pages):
            s0 = lax.dot_general(q, head_rows(k_refs[i], 0), _NT, preferred_element_type=f32)
            s1 = lax.dot_general(q, head_rows(k_refs[i], 1), _NT, preferred_element_type=f32)
            parts.append(jnp.where(first_group, s0, s1) * scale)

        def values(p):
            pb = p.astype(bf16)
            out = None
            for i in range(pages):
                pi = pb[:, i * PAGE_SIZE:(i + 1) * PAGE_SIZE]
                t = jnp.where(first_group,
                              jnp.dot(pi, head_rows(v_refs[i], 0), preferred_element_type=f32),
                              jnp.dot(pi, head_rows(v_refs[i], 1), preferred_element_type=f32))
                out = t if out is None else out + t
            return out

        online(jnp.concatenate(parts, axis=1), sel_ref[...] > 0.5, values)

    @pl.when(s == n_steps)
    def _():
        q = qa_ref[...].astype(f32)
        kn = kn_ref[...]
        vn = vn_ref[...]
        k_row = jnp.where(first_group, kn[:, :A_HEAD_DIM], kn[:, A_HEAD_DIM:])
        v_row = jnp.where(first_group, vn[:, :A_HEAD_DIM], vn[:, A_HEAD_DIM:])
        sm = jnp.sum(q * k_row, axis=1, keepdims=True) * scale
        online(sm, sel_ref[:, 0:1] > 0.5, lambda p: p * v_row)
        oa_ref[...] = acc_ref[...] / l_ref[...]


def _sweep2(qa, kn, vn, sel, cache_k, cache_v, page_table, layer, *, pages):
    S, n_pages = page_table.shape
    n_steps = n_pages // pages
    width = pages * PAGE_SIZE
    in_specs = [_per_seq_spec(qa), _per_seq_spec(kn), _per_seq_spec(vn),
                pl.BlockSpec((None, None, 1, width), lambda b, s, pt: (b, s, 0, 0))]
    for cache in (cache_k, cache_v):
        in_specs += [_page_spec(cache, layer, i, pages, n_steps) for i in range(pages)]
    grid_spec = pltpu.PrefetchScalarGridSpec(
        num_scalar_prefetch=1, grid=(S, n_steps + 1), in_specs=in_specs,
        out_specs=pl.BlockSpec((None, _ROW_PAD, A_HEAD_DIM), lambda b, s, pt: (b, 0, 0)),
        scratch_shapes=[pltpu.VMEM((_ROW_PAD, 1), f32), pltpu.VMEM((_ROW_PAD, 1), f32),
                        pltpu.VMEM((_ROW_PAD, A_HEAD_DIM), f32)])
    return pl.pallas_call(
        functools.partial(_sweep2_kernel, pages=pages, n_steps=n_steps),
        grid_spec=grid_spec,
        out_shape=jax.ShapeDtypeStruct((S, _ROW_PAD, A_HEAD_DIM), f32),
        compiler_params=_params(("parallel", "arbitrary")),
        name="decode_dsa",
    )(page_table, qa, kn, vn, sel, *([cache_k] * pages), *([cache_v] * pages))


def _rope_tables(pos):
    def cs(half):
        inv = ROPE_THETA ** (-jnp.arange(half, dtype=f32) / half)
        ang = pos.astype(f32)[:, None] * inv[None, :]
        return jnp.cos(ang), jnp.sin(ang)

    c, s = cs(A_HEAD_DIM // 2)
    cos_a = jnp.concatenate([c, c], axis=1)
    sin_a = jnp.concatenate([-s, s], axis=1)
    c, s = cs(IDX_DIM // 2)
    z = jnp.zeros_like(s)
    cos_i = jnp.concatenate([c, c, c, c], axis=1)
    sin_lo = jnp.concatenate([-s, z, -s, z], axis=1)
    sin_hi = jnp.concatenate([z, s, z, s], axis=1)
    return cos_a, sin_a, cos_i, sin_lo, sin_hi


def _layer_weights(l, w_in, attn_norm, c_q_norm, c_w_uq, c_kv_norm, c_w_uk, c_w_uv, b_v_norm, b_w_s, b_bias,
                   w_br_a, w_br_b, w_br_c, w_out, ffn_norm, ffn_w_up, ffn_conv_w, ffn_conv_b, ffn_w_down):
    w = w_in[l]
    o_ki = A_WIDTH + 2 * A_KV_WIDTH + IDX_WIDTH
    o_wi = o_ki + IDX_DIM
    o_u = o_wi + IDX_HEADS
    o_cq = o_u + 2 * B_WIDTH
    o_kr = o_cq + Q_LORA + KV_LORA
    o_gl = o_kr + C_ROPE
    w_all = jnp.concatenate(
        [w[:, :o_ki], w[:, o_ki:o_wi], w[:, o_kr:o_gl],
         jnp.pad(w[:, o_wi:o_u], ((0, 0), (0, LANES - IDX_HEADS))), w[:, o_u:o_cq], w[:, o_cq:o_kr]], axis=1)
    uq = c_w_uq[l]
    return {
        'attn_norm': attn_norm[l][None], 'w_all': w_all.astype(bf16), 'w_gate': w[:, o_gl:].astype(bf16),
        'c_q_norm': c_q_norm[l][None], 'c_kv_norm': c_kv_norm[l][None], 'b_v_norm': b_v_norm[l][None],
        'c_w_uq': jnp.concatenate([uq[:, :, :C_NOPE].reshape(Q_LORA, C_HEADS * C_NOPE),
                                   uq[:, :, C_NOPE:].reshape(Q_LORA, C_HEADS * C_ROPE)], axis=1).astype(bf16),
        'c_w_uk': jnp.transpose(c_w_uk[l], (1, 2, 0)).astype(bf16),
        'c_w_uv': jnp.transpose(c_w_uv[l], (1, 0, 2)).astype(bf16),
        'b_w_s': b_w_s[l], 'b_bias_t': b_bias[l].T,
        'b_w_00': jnp.repeat(b_w_s[l][:, 0, 0], B_GROUP_DIM)[None], 'b_bias_0': jnp.repeat(b_bias[l][:, 0], B_GROUP_DIM)[None],
        'w_br_a': w_br_a[l].astype(bf16), 'w_br_b': w_br_b[l].astype(bf16), 'w_br_c': w_br_c[l].astype(bf16),
        'w_out': w_out[l].astype(bf16), 'ffn_norm': ffn_norm[l][None],
        'ffn_w_a': ffn_w_up[l][:, :D_FF].astype(bf16), 'ffn_w_g': ffn_w_up[l][:, D_FF:].astype(bf16),
        'ffn_conv_w': ffn_conv_w[l], 'ffn_conv_b': ffn_conv_b[l][None], 'ffn_w_down': ffn_w_down[l].astype(bf16),
    }


def _pad_rows(a):
    return jnp.pad(a, ((0, 0), (0, _ROW_PAD - a.shape[1]), (0, 0)))


TM_PROJ = 256
TM_MIX = 256
TM_FFN = 256
TQ_DSA = 256
TQ_MLA = 128
CAUSAL_BUCKETS = 8
SWEEP_PAGES = 32


def kernel(x_prompt, x_sample, cache_a_k, cache_a_v, cache_a_idxk, cache_c_latent, cache_c_krope, state_ffn_conv, page_table, attn_norm, w_in, c_q_norm, c_w_uq, c_kv_norm, c_w_uk, c_w_uv, b_v_norm, b_w_s, b_bias, w_br_a, w_br_b, w_br_c, w_out, ffn_norm, ffn_w_up, ffn_conv_w, ffn_conv_b, ffn_w_down, final_norm):
    bsz, seq, _ = x_prompt.shape
    dbsz, dseq, _ = x_sample.shape
    depth = w_in.shape[0]
    assert dseq == 1 and seq % CHUNK == 0
    n_pages = page_table.shape[1]
    past_len = n_pages * PAGE_SIZE
    assert past_len % CHUNK == 0
    topk_p = min(TOPK_MAX, seq // 4)
    topk_s = min(TOPK_MAX, (past_len + dseq) // 4)
    n_buckets = min(CAUSAL_BUCKETS, seq // max(TQ_DSA, TQ_MLA, topk_p))
    pages = min(SWEEP_PAGES, n_pages)
    tabs_p = _rope_tables(jnp.arange(seq, dtype=jnp.int32))
    tabs_s = _rope_tables(jnp.full((dbsz,), past_len, dtype=jnp.int32))
    fin = final_norm[None]
    cache_k2 = cache_a_k.reshape(cache_a_k.shape[:2] + (PAGE_SIZE * A_KV_HEADS, A_HEAD_DIM))
    cache_v2 = cache_a_v.reshape(cache_a_v.shape[:2] + (PAGE_SIZE * A_KV_HEADS, A_HEAD_DIM))
    cache_idx_t = jnp.swapaxes(cache_a_idxk, 2, 3)
    cache_kro_t = jnp.swapaxes(cache_c_krope, 2, 3)

    xp = x_prompt
    xs = x_sample.reshape(1, dbsz, D_MODEL)
    outs = {k: [] for k in ('pk', 'pv', 'pik', 'pcl', 'pcr', 'pfc', 'sk', 'sv', 'sik', 'scl', 'scr', 'sfc', 'sbv')}
    yp = ys = None
    for l in range(depth):
        lw = _layer_weights(l, w_in, attn_norm, c_q_norm, c_w_uq, c_kv_norm, c_w_uk, c_w_uv, b_v_norm, b_w_s,
                            b_bias, w_br_a, w_br_b, w_br_c, w_out, ffn_norm, ffn_w_up, ffn_conv_w, ffn_conv_b,
                            ffn_w_down)
        final = l == depth - 1

        pr = _project(xp, dict(lw, gate_w=lw['b_w_s'], gate_b=lw['b_bias_t']), tabs_p, tm=TM_PROJ, decode=False)
        oa = _dsa_prompt(pr, tq=TQ_DSA, topk=topk_p, n_buckets=n_buckets)
        ol = _mla_prompt(pr, tq=TQ_MLA, n_buckets=n_buckets)
        x1 = _mixer(xp, oa, pr['ob'], ol, lw, tm=TM_MIX)
        res = _ffn(x1, lw, fin, tm=TM_FFN, final=final)
        xp = res[0]
        if final:
            yp = res[2]
        outs['pk'].append(pr['ka'].reshape(bsz, seq, A_KV_HEADS, A_HEAD_DIM))
        outs['pv'].append(pr['va'].reshape(bsz, seq, A_KV_HEADS, A_HEAD_DIM))
        outs['pik'].append(pr['ki']); outs['pcl'].append(pr['ckv']); outs['pcr'].append(pr['kr'])
        outs['pfc'].append(res[1])

        sr = _project(xs, dict(lw, gate_w=lw['b_w_00'], gate_b=lw['b_bias_0']), tabs_s, tm=dbsz, decode=True)
        seqs = lambda a, r: a.reshape(dbsz, r, a.shape[-1] // r)
        sweep_in = (_pad_rows(seqs(sr['qi'], IDX_HEADS)),
                    _pad_rows(sr['wi'][0, :, :IDX_HEADS, None]),
                    _pad_rows(seqs(sr['qlat'], C_HEADS)), _pad_rows(seqs(sr['qrope'], C_HEADS)),
                    seqs(sr['ki'], 1), seqs(sr['ckv'], 1), seqs(sr['kr'], 1))
        scores, ol_s = _sweep1(sweep_in, (cache_idx_t, cache_c_latent, cache_kro_t), page_table, l, pages=pages)
        sel = _select(scores.reshape(dbsz, -1), topk=topk_s).reshape(scores.shape)
        oa_s = _sweep2(_pad_rows(seqs(sr['qa'], A_HEADS)), seqs(sr['ka'], 1), seqs(sr['va'], 1), sel,
                       cache_k2, cache_v2, page_table, l, pages=pages)
        oa_s = oa_s[:, :A_HEADS].reshape(1, dbsz, A_WIDTH).astype(bf16)
        ol_s = ol_s[:, :C_HEADS].reshape(1, dbsz, C_HEADS * KV_LORA).astype(bf16)
        x1s = _mixer(xs, oa_s, sr['ob'], ol_s, lw, tm=dbsz)
        hist = state_ffn_conv[l]
        res_s = _ffn(x1s, lw, fin, tm=dbsz, final=final,
                     hist=(hist[None, :, 0], hist[None, :, 1]))
        xs = res_s[0]
        if final:
            ys = res_s[2]
        outs['sk'].append(sr['ka'].reshape(dbsz, dseq, A_KV_HEADS, A_HEAD_DIM))
        outs['sv'].append(sr['va'].reshape(dbsz, dseq, A_KV_HEADS, A_HEAD_DIM))
        outs['sik'].append(sr['ki'].reshape(dbsz, dseq, IDX_DIM))
        outs['scl'].append(sr['ckv'].reshape(dbsz, dseq, KV_LORA))
        outs['scr'].append(sr['kr'].reshape(dbsz, dseq, C_ROPE))
        outs['sfc'].append(jnp.stack([hist[:, 1], res_s[1][0]], axis=1))
        outs['sbv'].append(sr['vn'].reshape(dbsz, dseq, B_WIDTH))

    st = {k: jnp.stack(v) for k, v in outs.items()}
    return (yp, ys.reshape(dbsz, dseq, D_MODEL), st['pk'], st['pv'], st['pik'], st['pcl'], st['pcr'], st['pfc'],
            st['sk'], st['sv'], st['sik'], st['scl'], st['scr'], st['sfc'], st['sbv'])
```

```python
import functools

import jax
import jax.numpy as jnp
from jax import lax
from jax.experimental import pallas as pl
from jax.experimental.pallas import tpu as pltpu

D_MODEL = 1024
PAGE_SIZE = 128
A_HEADS = 4
A_KV_HEADS = 2
A_HEAD_DIM = 128
A_WIDTH = A_HEADS * A_HEAD_DIM
A_KV_WIDTH = A_KV_HEADS * A_HEAD_DIM
IDX_HEADS = 8
IDX_DIM = 64
IDX_WIDTH = IDX_HEADS * IDX_DIM
TOPK_MAX = 256
CHUNK = 128
B_GROUPS = 4
B_GROUP_DIM = 128
B_WIDTH = B_GROUPS * B_GROUP_DIM
C_HEADS = 4
Q_LORA = 256
KV_LORA = 256
C_NOPE = 128
C_ROPE = 64
C_VDIM = 128
C_WIDTH = C_HEADS * C_VDIM
N_BRANCH = 3
D_FF = 2816
CONV_W = 3
ROPE_THETA = 10000.0
EPS = 1e-6

LANES = 128
SUBLANES = 8
VMEM_LIMIT = 56 * 1024 * 1024

_O_QA = 0
_O_KA = _O_QA + A_WIDTH
_O_VA = _O_KA + A_KV_WIDTH
_O_QI = _O_VA + A_KV_WIDTH
_O_KIKR = _O_QI + IDX_WIDTH
_O_WI = _O_KIKR + LANES
_O_U = _O_WI + LANES
_O_V = _O_U + B_WIDTH
_O_CQ = _O_V + B_WIDTH
_O_CKV = _O_CQ + Q_LORA
_W_ALL = _O_CKV + KV_LORA

_NT = (((1,), (1,)), ((), ()))
_INT_MIN = -2147483648
_NEG_INF_KEY = -2139095041
_ROW_PAD = 16
_SEARCH_GROUP_ROWS = 64

f32 = jnp.float32
bf16 = jnp.bfloat16


def _rms(x, g):
    return x * lax.rsqrt(jnp.mean(x * x, axis=-1, keepdims=True) + EPS) * g


def _rope_full(x, cos, sin):
    parts = []
    for k in range(x.shape[1] // LANES):
        xs = x[:, k * LANES:(k + 1) * LANES]
        parts.append(xs * cos + pltpu.roll(xs, LANES // 2, 1) * sin)
    return parts[0] if len(parts) == 1 else jnp.concatenate(parts, axis=1)


def _rope_half(x, cos, sin_lo, sin_hi):
    parts = []
    for k in range(x.shape[1] // LANES):
        xs = x[:, k * LANES:(k + 1) * LANES]
        parts.append(xs * cos + pltpu.roll(xs, LANES - IDX_DIM // 2, 1) * sin_lo
                     + pltpu.roll(xs, IDX_DIM // 2, 1) * sin_hi)
    return parts[0] if len(parts) == 1 else jnp.concatenate(parts, axis=1)


def _order_key(score):
    bits = lax.bitcast_convert_type(score, jnp.int32)
    return bits ^ ((bits >> 31) & jnp.int32(0x7FFFFFFF))


def _count(mask):
    return jnp.sum(jnp.where(mask, 1.0, 0.0), axis=1, keepdims=True)


def _kth_largest_key(key, k):
    rows = key.shape[0]
    group = min(rows, _SEARCH_GROUP_ROWS)
    parts = [key[r:r + group] for r in range(0, rows, group)]

    def body(i, tbs):
        bit = jnp.left_shift(jnp.int32(1), 31 - i)
        out = []
        for part, tb in zip(parts, tbs):
            cand_b = tb | bit
            cnt = _count(part >= (cand_b ^ jnp.int32(_INT_MIN)))
            out.append(jnp.where(cnt >= k, cand_b, tb))
        return tuple(out)

    tbs = lax.fori_loop(0, 32, body, tuple(jnp.zeros((group, 1), jnp.int32) for _ in parts))
    tb = tbs[0] if len(tbs) == 1 else jnp.concatenate(tbs, axis=0)
    return tb ^ jnp.int32(_INT_MIN)


def _topk_membership(key, k, sel_ref):
    rows, n = key.shape
    thr = _kth_largest_key(key, k)
    gt = key > thr
    eq = key == thr
    need = k - _count(gt)
    sel_ref[...] = jnp.where(gt | eq, 1.0, 0.0)
    tie = (thr > _NEG_INF_KEY) & (_count(eq) > need)

    @pl.when(jnp.max(jnp.where(tie, 1.0, 0.0)) > 0.0)
    def _():
        r = lax.broadcasted_iota(jnp.int32, (LANES, LANES), 0)
        c = lax.broadcasted_iota(jnp.int32, (LANES, LANES), 1)
        upper = jnp.where(r < c, 1.0, 0.0).astype(bf16)
        run = jnp.zeros((rows, 1), f32)
        for j in range(n // LANES):
            sl = slice(j * LANES, (j + 1) * LANES)
            eq_j = jnp.where(eq[:, sl], 1.0, 0.0)
            before = jnp.dot(eq_j.astype(bf16), upper, preferred_element_type=f32) + run
            keep = gt[:, sl] | (eq[:, sl] & (before < need))
            sel_ref[:, sl] = jnp.where(keep, 1.0, 0.0)
            run = run + jnp.sum(eq_j, axis=1, keepdims=True)


def _softmax_rows(s):
    p = jnp.exp(s - jnp.max(s, axis=1, keepdims=True))
    return p / jnp.sum(p, axis=1, keepdims=True)


def _const_spec(shape):
    nd = len(shape)
    return pl.BlockSpec(shape, lambda *_: (0,) * nd, pipeline_mode=pl.Buffered(1))


def _params(sem):
    return pltpu.CompilerParams(dimension_semantics=sem, vmem_limit_bytes=VMEM_LIMIT)


def _causal_buckets(j, tq, L, n_buckets, body):
    step = L // n_buckets
    per = step // tq
    for bk in range(n_buckets):
        @pl.when((j >= bk * per) & (j < (bk + 1) * per))
        def _(bk=bk):
            body((bk + 1) * step)


_PROJ_OUTS = (('qa', A_WIDTH, bf16), ('ka', A_KV_WIDTH, f32), ('va', A_KV_WIDTH, f32), ('qi', IDX_WIDTH, bf16),
              ('ki', IDX_DIM, f32), ('kr', C_ROPE, f32), ('wi', LANES, f32), ('ob', B_WIDTH, bf16),
              ('ckv', KV_LORA, f32), ('qlat', C_HEADS * KV_LORA, bf16), ('qrope', C_HEADS * C_ROPE, bf16))
_PROJ_OUTS_PROMPT = (('ka_b', A_KV_WIDTH, bf16), ('va_b', A_KV_WIDTH, bf16), ('ki_b', IDX_DIM, bf16),
                     ('ckv_b', KV_LORA, bf16), ('kr_b', C_ROPE, bf16))
_PROJ_OUTS_DECODE = (('vn', B_WIDTH, f32),)


def _proj_kernel(x_ref, gain_ref, w_ref, cqn_ref, wuq_ref, wuk_ref, ckvn_ref, bvn_ref, ws_ref, bias_ref,
                 cosa_ref, sina_ref, cosi_ref, silo_ref, sihi_ref, *out_refs, decode):
    names = [n for n, _, _ in _PROJ_OUTS + (_PROJ_OUTS_DECODE if decode else _PROJ_OUTS_PROMPT)]
    o = dict(zip(names, out_refs))
    tm = x_ref.shape[0]
    h = _rms(x_ref[...], gain_ref[...]).astype(bf16)
    z = jnp.dot(h, w_ref[...], preferred_element_type=f32)
    cosa, sina = cosa_ref[...], sina_ref[...]
    cosi, silo, sihi = cosi_ref[...], silo_ref[...], sihi_ref[...]

    o['qa'][...] = _rope_full(z[:, _O_QA:_O_KA], cosa, sina).astype(bf16)
    ka = _rope_full(z[:, _O_KA:_O_VA], cosa, sina)
    va = z[:, _O_VA:_O_QI]
    o['ka'][...] = ka
    o['va'][...] = va
    o['qi'][...] = _rope_half(z[:, _O_QI:_O_KIKR], cosi, silo, sihi).astype(bf16)
    kikr = _rope_half(z[:, _O_KIKR:_O_WI], cosi, silo, sihi)
    o['ki'][...] = kikr[:, :IDX_DIM]
    o['kr'][...] = kikr[:, IDX_DIM:]
    o['wi'][...] = z[:, _O_WI:_O_U] * IDX_HEADS ** -0.5
    ckv = _rms(z[:, _O_CKV:_W_ALL], ckvn_ref[...])
    o['ckv'][...] = ckv
    if not decode:
        o['ka_b'][...] = ka.astype(bf16)
        o['va_b'][...] = va.astype(bf16)
        kikr_b = kikr.astype(bf16)
        o['ki_b'][...] = kikr_b[:, :IDX_DIM]
        o['kr_b'][...] = kikr_b[:, IDX_DIM:]
        o['ckv_b'][...] = ckv.astype(bf16)

    u = jax.nn.gelu(z[:, _O_U:_O_V])
    v = _rms(jax.nn.gelu(z[:, _O_V:_O_CQ]), bvn_ref[...])
    if decode:
        o['vn'][...] = v
        o['ob'][...] = (u * (ws_ref[...] * v + bias_ref[...])).astype(bf16)
    else:
        r = lax.broadcasted_iota(jnp.int32, (CHUNK, CHUNK), 0)
        c = lax.broadcasted_iota(jnp.int32, (CHUNK, CHUNK), 1)
        vb = v.astype(bf16)
        bias = bias_ref[...]
        for g in range(B_GROUPS):
            w_g = jnp.where(r >= c, ws_ref[g], 0.0).astype(bf16)
            cols = slice(g * B_GROUP_DIM, (g + 1) * B_GROUP_DIM)
            for k in range(tm // CHUNK):
                rows = slice(k * CHUNK, (k + 1) * CHUNK)
                mixed = jnp.dot(w_g, vb[rows, cols], preferred_element_type=f32) + bias[:, g:g + 1]
                o['ob'][rows, cols] = (u[rows, cols] * mixed).astype(bf16)

    cq = _rms(z[:, _O_CQ:_O_CKV], cqn_ref[...]).astype(bf16)
    qc = jnp.dot(cq, wuq_ref[...], preferred_element_type=f32)
    for hd in range(C_HEADS):
        q_nope = qc[:, hd * C_NOPE:(hd + 1) * C_NOPE].astype(bf16)
        o['qlat'][:, hd * KV_LORA:(hd + 1) * KV_LORA] = jnp.dot(
            q_nope, wuk_ref[hd], preferred_element_type=f32).astype(bf16)
    o['qrope'][...] = _rope_half(qc[:, C_HEADS * C_NOPE:], cosi, silo, sihi).astype(bf16)


def _project(x, lw, tabs, *, tm, decode):
    B, L, _ = x.shape
    outs = _PROJ_OUTS + (_PROJ_OUTS_DECODE if decode else _PROJ_OUTS_PROMPT)

    def tok(width):
        return pl.BlockSpec((None, tm, width), lambda b, i: (b, i, 0))

    weights = [lw['attn_norm'], lw['w_all'], lw['c_q_norm'], lw['c_w_uq'], lw['c_w_uk'], lw['c_kv_norm'],
               lw['b_v_norm'], lw['gate_w'], lw['gate_b']]
    in_specs = ([tok(D_MODEL)] + [_const_spec(w.shape) for w in weights]
                + [pl.BlockSpec((tm, LANES), lambda b, i: (i, 0)) for _ in range(5)])
    res = pl.pallas_call(
        functools.partial(_proj_kernel, decode=decode),
        grid=(B, L // tm), in_specs=in_specs,
        out_specs=[tok(w) for _, w, _ in outs],
        out_shape=[jax.ShapeDtypeStruct((B, L, w), dt) for _, w, dt in outs],
        compiler_params=_params(("parallel", "parallel")),
        name="proj_decode" if decode else "proj_prompt",
    )(x, *weights, *tabs)
    return dict(zip([n for n, _, _ in outs], res))


def _dsa_kernel(qi_ref, wi_ref, ki_ref, qa_ref, ka_ref, va_ref, oa_ref, sel_ref, *, topk, first_block):
    tq, ext = sel_ref.shape
    j = first_block + pl.program_id(1)
    rep = A_HEADS // A_KV_HEADS
    ki = ki_ref[...]
    wi = wi_ref[...] * IDX_DIM ** -0.5
    score = jnp.zeros((tq, ext), f32)
    for hd in range(IDX_HEADS):
        s = lax.dot_general(qi_ref[:, hd * IDX_DIM:(hd + 1) * IDX_DIM], ki, _NT, preferred_element_type=f32)
        score = score + wi[:, hd:hd + 1] * jnp.maximum(s, 0.0)
    qpos = j * tq + lax.broadcasted_iota(jnp.int32, (tq, ext), 0)
    kpos = lax.broadcasted_iota(jnp.int32, (tq, ext), 1)
    key = _order_key(jnp.where(kpos <= qpos, score, -jnp.inf))
    _topk_membership(key, topk, sel_ref)
    sel = (sel_ref[...] > 0.5) & (key > _NEG_INF_KEY)
    sel2 = jnp.concatenate([sel] * rep, axis=0)
    for g in range(A_KV_HEADS):
        kv = slice(g * A_HEAD_DIM, (g + 1) * A_HEAD_DIM)
        q = jnp.concatenate([qa_ref[:, (g * rep + r) * A_HEAD_DIM:(g * rep + r + 1) * A_HEAD_DIM]
                             for r in range(rep)], axis=0)
        s = lax.dot_general(q, ka_ref[:, kv], _NT, preferred_element_type=f32) * A_HEAD_DIM ** -0.5
        p = _softmax_rows(jnp.where(sel2, s, -jnp.inf))
        out = jnp.dot(p.astype(bf16), va_ref[:, kv], preferred_element_type=f32).astype(bf16)
        for r in range(rep):
            hd = g * rep + r
            oa_ref[:, hd * A_HEAD_DIM:(hd + 1) * A_HEAD_DIM] = out[r * tq:(r + 1) * tq]


def _dsa_prompt(pr, *, tq, topk, n_buckets):
    B, L, _ = pr['qa'].shape
    step = L // n_buckets
    per = step // tq
    pieces = []
    for bk in range(n_buckets):
        ext = (bk + 1) * step

        def qspec(width, bk=bk):
            return pl.BlockSpec((None, tq, width), lambda b, j: (b, bk * per + j, 0))

        def kspec(width, ext=ext):
            return pl.BlockSpec((None, ext, width), lambda b, j: (b, 0, 0))

        pieces.append(pl.pallas_call(
            functools.partial(_dsa_kernel, topk=topk, first_block=bk * per),
            grid=(B, per),
            in_specs=[qspec(IDX_WIDTH), qspec(LANES), kspec(IDX_DIM), qspec(A_WIDTH), kspec(A_KV_WIDTH),
                      kspec(A_KV_WIDTH)],
            out_specs=pl.BlockSpec((None, tq, A_WIDTH), lambda b, j: (b, j, 0)),
            out_shape=jax.ShapeDtypeStruct((B, step, A_WIDTH), bf16),
            scratch_shapes=[pltpu.VMEM((tq, ext), f32)],
            compiler_params=_params(("parallel", "parallel")),
            name=f"dsa_prompt_{ext}",
        )(pr['qi'], pr['wi'], pr['ki_b'], pr['qa'], pr['ka_b'], pr['va_b']))
    return jnp.concatenate(pieces, axis=1)


def _mla_kernel(ql_ref, qr_ref, c_ref, kr_ref, ol_ref, *, n_buckets):
    tq = ql_ref.shape[0]
    L = c_ref.shape[0]
    j = pl.program_id(1)

    def body(ext):
        c = c_ref[0:ext, :]
        ql = jnp.concatenate([ql_ref[:, hd * KV_LORA:(hd + 1) * KV_LORA] for hd in range(C_HEADS)], axis=0)
        qr = jnp.concatenate([qr_ref[:, hd * C_ROPE:(hd + 1) * C_ROPE] for hd in range(C_HEADS)], axis=0)
        s = (lax.dot_general(ql, c, _NT, preferred_element_type=f32)
             + lax.dot_general(qr, kr_ref[0:ext, :], _NT, preferred_element_type=f32))
        row = lax.broadcasted_iota(jnp.int32, (C_HEADS * tq, ext), 0)
        qpos = j * tq + (row & (tq - 1))
        kpos = lax.broadcasted_iota(jnp.int32, (C_HEADS * tq, ext), 1)
        p = _softmax_rows(jnp.where(kpos <= qpos, s * (C_NOPE + C_ROPE) ** -0.5, -jnp.inf))
        out = jnp.dot(p.astype(bf16), c, preferred_element_type=f32).astype(bf16)
        for hd in range(C_HEADS):
            ol_ref[:, hd * KV_LORA:(hd + 1) * KV_LORA] = out[hd * tq:(hd + 1) * tq]

    _causal_buckets(j, tq, L, n_buckets, body)


def _mla_prompt(pr, *, tq, n_buckets):
    B, L, _ = pr['qlat'].shape
    assert tq & (tq - 1) == 0

    def qspec(width):
        return pl.BlockSpec((None, tq, width), lambda b, j: (b, j, 0))

    def kspec(width):
        return pl.BlockSpec((None, L, width), lambda b, j: (b, 0, 0))

    return pl.pallas_call(
        functools.partial(_mla_kernel, n_buckets=n_buckets),
        grid=(B, L // tq),
        in_specs=[qspec(C_HEADS * KV_LORA), qspec(C_HEADS * C_ROPE), kspec(KV_LORA), kspec(C_ROPE)],
        out_specs=qspec(C_HEADS * KV_LORA),
        out_shape=jax.ShapeDtypeStruct((B, L, C_HEADS * KV_LORA), bf16),
        compiler_params=_params(("parallel", "parallel")),
        name="mla_prompt",
    )(pr['qlat'], pr['qrope'], pr['ckv_b'], pr['kr_b'])


def _mix_kernel(x_ref, oa_ref, ob_ref, ol_ref, gain_ref, wg_ref, wa_ref, wb_ref, wc_ref, wuv_ref, wo_ref, x1_ref):
    x = x_ref[...]
    h = _rms(x, gain_ref[...]).astype(bf16)
    oc = jnp.concatenate(
        [jnp.dot(ol_ref[:, hd * KV_LORA:(hd + 1) * KV_LORA], wuv_ref[hd], preferred_element_type=f32)
         for hd in range(C_HEADS)], axis=1).astype(bf16)
    merged = None
    for k, (o, w_ref) in enumerate(((oa_ref[...], wa_ref), (ob_ref[...], wb_ref), (oc, wc_ref))):
        gate = jax.nn.sigmoid(jnp.dot(h, wg_ref[:, k * D_MODEL:(k + 1) * D_MODEL], preferred_element_type=f32))
        term = gate * jnp.dot(o, w_ref[...], preferred_element_type=f32)
        merged = term if merged is None else merged + term
    x1_ref[...] = x + jnp.dot(merged.astype(bf16), wo_ref[...], preferred_element_type=f32)


def _mixer(x, oa, ob, ol, lw, *, tm):
    B, L, _ = x.shape

    def tok(width):
        return pl.BlockSpec((None, tm, width), lambda b, i: (b, i, 0))

    weights = [lw['attn_norm'], lw['w_gate'], lw['w_br_a'], lw['w_br_b'], lw['w_br_c'], lw['c_w_uv'], lw['w_out']]
    return pl.pallas_call(
        _mix_kernel,
        grid=(B, L // tm),
        in_specs=[tok(D_MODEL), tok(A_WIDTH), tok(B_WIDTH), tok(C_HEADS * KV_LORA)]
        + [_const_spec(w.shape) for w in weights],
        out_specs=tok(D_MODEL),
        out_shape=jax.ShapeDtypeStruct((B, L, D_MODEL), f32),
        compiler_params=_params(("parallel", "parallel")),
        name="mixer",
    )(x, oa, ob, ol, *weights)


def _ffn_kernel(*refs, decode, final):
    if decode:
        x1_ref, gain_ref, wa_ref, wg_ref, cw_ref, cb_ref, wd_ref, fin_ref, h0_ref, h1_ref = refs[:10]
        outs = refs[10:]
    else:
        x1_ref, gain_ref, wa_ref, wg_ref, cw_ref, cb_ref, wd_ref, fin_ref = refs[:8]
        outs = refs[8:]
    x2_ref, hist_ref = outs[0], outs[1]
    tm = x1_ref.shape[0]
    x1 = x1_ref[...]
    h = _rms(x1, gain_ref[...]).astype(bf16)
    a = jnp.dot(h, wa_ref[...], preferred_element_type=f32)
    g = jnp.dot(h, wg_ref[...], preferred_element_type=f32)
    cw = cw_ref[...]
    if decode:
        hist_ref[...] = g
        conv = g * cw[2:3] + cb_ref[...] + h0_ref[...] * cw[0:1] + h1_ref[...] * cw[1:2]
    else:
        gbuf = outs[-1]

        @pl.when(pl.program_id(1) == 0)
        def _():
            gbuf[0:SUBLANES, :] = jnp.zeros((SUBLANES, D_FF), f32)

        gbuf[SUBLANES:SUBLANES + tm, :] = g
        conv = g * cw[2:3] + cb_ref[...]
        for k in range(CONV_W - 1):
            off = SUBLANES - (CONV_W - 1) + k
            conv = conv + gbuf[off:off + tm, :] * cw[k:k + 1]
        hist_ref[...] = g[tm - (CONV_W - 1):, :]
        gbuf[0:SUBLANES, :] = g[tm - SUBLANES:, :]
    y = jnp.dot((jax.nn.silu(conv) * a).astype(bf16), wd_ref[...], preferred_element_type=f32)
    x2 = x1 + y
    x2_ref[...] = x2
    if final:
        outs[2][...] = _rms(x2, fin_ref[...])


def _ffn(x1, lw, final_norm, *, tm, final, hist=None):
    B, L, _ = x1.shape
    decode = hist is not None

    def tok(width):
        return pl.BlockSpec((None, tm, width), lambda b, i: (b, i, 0))

    weights = [lw['ffn_norm'], lw['ffn_w_a'], lw['ffn_w_g'], lw['ffn_conv_w'], lw['ffn_conv_b'], lw['ffn_w_down'],
               final_norm]
    in_specs = [tok(D_MODEL)] + [_const_spec(w.shape) for w in weights]
    args = [x1, *weights]
    out_specs = [tok(D_MODEL)]
    out_shape = [jax.ShapeDtypeStruct((B, L, D_MODEL), f32)]
    scratch = []
    if decode:
        in_specs += [tok(D_FF), tok(D_FF)]
        args += list(hist)
        out_specs.append(tok(D_FF))
        out_shape.append(jax.ShapeDtypeStruct((B, L, D_FF), f32))
    else:
        out_specs.append(pl.BlockSpec((None, CONV_W - 1, D_FF), lambda b, i: (b, 0, 0)))
        out_shape.append(jax.ShapeDtypeStruct((B, CONV_W - 1, D_FF), f32))
        scratch.append(pltpu.VMEM((tm + SUBLANES, D_FF), f32))
    if final:
        out_specs.append(tok(D_MODEL))
        out_shape.append(jax.ShapeDtypeStruct((B, L, D_MODEL), f32))
    return pl.pallas_call(
        functools.partial(_ffn_kernel, decode=decode, final=final),
        grid=(B, L // tm), in_specs=in_specs, out_specs=out_specs, out_shape=out_shape,
        scratch_shapes=scratch,
        compiler_params=_params(("parallel", "arbitrary")),
        name="ffn_decode" if decode else "ffn_prompt",
    )(*args)


def _select_kernel(sc_ref, sel_ref, *, topk):
    _topk_membership(_order_key(sc_ref[...]), topk, sel_ref)


def _select(scores, *, topk):
    return pl.pallas_call(
        functools.partial(_select_kernel, topk=topk),
        out_shape=jax.ShapeDtypeStruct(scores.shape, f32),
        compiler_params=pltpu.CompilerParams(vmem_limit_bytes=VMEM_LIMIT),
        name="decode_select",
    )(scores)


def _page_copies(pt_ref, seq, first_page, i, layer, srcs, bufs, sems, slot, layouts):
    page = pt_ref[seq, first_page + i]
    out = []
    for k, (src, buf, layout) in enumerate(zip(srcs, bufs, layouts)):
        rows = src.shape[2]
        if layout == 'rows':
            dst = buf.at[slot, pl.ds(pl.multiple_of(i * rows, rows), rows), :]
        else:
            dst = buf.at[slot, :, pl.ds(pl.multiple_of(i * PAGE_SIZE, PAGE_SIZE), PAGE_SIZE)]
        out.append(pltpu.make_async_copy(src.at[layer, page], dst, sems.at[k, slot]))
    return out


def _paged_chunks(pt_ref, seq, layer, srcs, bufs, sems, layouts, chunk, n_chunks, compute):
    def each(c, slot, act):
        def body(i, carry):
            for cp in _page_copies(pt_ref, seq, c * chunk, i, layer, srcs, bufs, sems, slot, layouts):
                act(cp)
            return carry
        lax.fori_loop(0, chunk, body, 0)

    each(0, 0, lambda cp: cp.start())
    for c in range(n_chunks):
        slot = c % 2
        if c + 1 < n_chunks:
            each(c + 1, 1 - slot, lambda cp: cp.start())
        each(c, slot, lambda cp: cp.wait())
        compute(c, slot)


def _sweep1_kernel(pt_ref, qi_ref, wi_ref, ql_ref, qr_ref, kin_ref, cn_ref, krn_ref, idx_hbm, lat_hbm, kro_hbm,
                   sc_ref, scn_ref, ol_ref, idx_buf, lat_buf, kro_buf, sems, *, layer, chunk, n_chunks):
    seq = pl.program_id(0)
    width = chunk * PAGE_SIZE
    scale_c = (C_NOPE + C_ROPE) ** -0.5
    qi, ql, qr = qi_ref[...], ql_ref[...], qr_ref[...]
    wi = wi_ref[...] * IDX_DIM ** -0.5
    state = [jnp.full((_ROW_PAD, 1), -jnp.inf, f32), jnp.zeros((_ROW_PAD, 1), f32),
             jnp.zeros((_ROW_PAD, KV_LORA), f32)]

    def online(sm, values):
        m_old, l_old, acc = state
        m_new = jnp.maximum(m_old, jnp.max(sm, axis=1, keepdims=True))
        alpha = jnp.exp(m_old - m_new)
        p = jnp.exp(sm - m_new)
        state[:] = [m_new, alpha * l_old + jnp.sum(p, axis=1, keepdims=True), alpha * acc + values(p)]

    def compute(c, slot):
        si = jnp.dot(qi, idx_buf[slot].astype(bf16), preferred_element_type=f32)
        sc_ref[:, c * width:(c + 1) * width] = jnp.sum(wi * jnp.maximum(si, 0.0), axis=0, keepdims=True)
        lat = lat_buf[slot].astype(bf16)
        sm = (lax.dot_general(ql, lat, _NT, preferred_element_type=f32)
              + jnp.dot(qr, kro_buf[slot].astype(bf16), preferred_element_type=f32)) * scale_c
        online(sm, lambda p: jnp.dot(p.astype(bf16), lat, preferred_element_type=f32))

    _paged_chunks(pt_ref, seq, layer, (idx_hbm, lat_hbm, kro_hbm), (idx_buf, lat_buf, kro_buf), sems,
                  ('lanes', 'rows', 'lanes'), chunk, n_chunks, compute)

    s_new = jnp.sum(qi.astype(f32) * kin_ref[...], axis=1, keepdims=True)
    sc_new = jnp.sum(wi * jnp.maximum(s_new, 0.0), axis=0, keepdims=True)
    lane = lax.broadcasted_iota(jnp.int32, scn_ref.shape, 1)
    scn_ref[...] = jnp.where(lane == 0, sc_new, -jnp.inf)
    cn = cn_ref[...]
    sm = (jnp.sum(ql.astype(f32) * cn, axis=1, keepdims=True)
          + jnp.sum(qr.astype(f32) * krn_ref[...], axis=1, keepdims=True)) * scale_c
    online(sm, lambda p: p * cn)
    ol_ref[...] = state[2] / state[1]


def _seq_spec(a):
    nd = a.ndim - 1
    return pl.BlockSpec((None,) + a.shape[1:], lambda b, pt: (b,) + (0,) * nd)


def _sweep1(sr, caches, page_table, layer, *, chunk):
    S, n_pages = page_table.shape
    past_len = n_pages * PAGE_SIZE
    width = chunk * PAGE_SIZE
    grid_spec = pltpu.PrefetchScalarGridSpec(
        num_scalar_prefetch=1, grid=(S,),
        in_specs=[_seq_spec(a) for a in sr] + [pl.BlockSpec(memory_space=pl.ANY)] * len(caches),
        out_specs=[pl.BlockSpec((None, 1, past_len), lambda b, pt: (b, 0, 0)),
                   pl.BlockSpec((None, 1, LANES), lambda b, pt: (b, 0, 0)),
                   pl.BlockSpec((None, _ROW_PAD, KV_LORA), lambda b, pt: (b, 0, 0))],
        scratch_shapes=[pltpu.VMEM((2, IDX_DIM, width), f32), pltpu.VMEM((2, width, KV_LORA), f32),
                        pltpu.VMEM((2, C_ROPE, width), f32), pltpu.SemaphoreType.DMA((len(caches), 2))])
    return pl.pallas_call(
        functools.partial(_sweep1_kernel, layer=layer, chunk=chunk, n_chunks=n_pages // chunk),
        grid_spec=grid_spec,
        out_shape=[jax.ShapeDtypeStruct((S, 1, past_len), f32),
                   jax.ShapeDtypeStruct((S, 1, LANES), f32),
                   jax.ShapeDtypeStruct((S, _ROW_PAD, KV_LORA), f32)],
        compiler_params=_params(("arbitrary",)),
        name="decode_indexer_mla",
    )(page_table, *sr, *caches)


def _sweep2_kernel(pt_ref, qa_ref, kn_ref, vn_ref, sel_ref, seln_ref, k_hbm, v_hbm, oa_ref, k_buf, v_buf, sems,
                   *, layer, chunk, n_chunks):
    seq = pl.program_id(0)
    width = chunk * PAGE_SIZE
    rep = A_HEADS // A_KV_HEADS
    row = lax.broadcasted_iota(jnp.int32, (_ROW_PAD, 1), 0)
    first_group = row < rep
    scale = A_HEAD_DIM ** -0.5
    q = qa_ref[...]
    state = [jnp.full((_ROW_PAD, 1), -jnp.inf, f32), jnp.zeros((_ROW_PAD, 1), f32),
             jnp.zeros((_ROW_PAD, A_HEAD_DIM), f32)]

    def online(sm, keep, values):
        m_old, l_old, acc = state
        m_new = jnp.maximum(m_old, jnp.max(jnp.where(keep, sm, -jnp.inf), axis=1, keepdims=True))
        m_use = jnp.where(m_new == -jnp.inf, 0.0, m_new)
        alpha = jnp.exp(m_old - m_use)
        p = jnp.where(keep, jnp.exp(sm - m_use), 0.0)
        state[:] = [m_new, alpha * l_old + jnp.sum(p, axis=1, keepdims=True), alpha * acc + values(p)]

    def head_rows(buf, slot, g):
        return buf[slot, pl.ds(g, width, stride=A_KV_HEADS), :].astype(bf16)

    def compute(c, slot):
        s0 = lax.dot_general(q, head_rows(k_buf, slot, 0), _NT, preferred_element_type=f32)
        s1 = lax.dot_general(q, head_rows(k_buf, slot, 1), _NT, preferred_element_type=f32)

        def values(p):
            pb = p.astype(bf16)
            return jnp.where(first_group,
                             jnp.dot(pb, head_rows(v_buf, slot, 0), preferred_element_type=f32),
                             jnp.dot(pb, head_rows(v_buf, slot, 1), preferred_element_type=f32))

        online(jnp.where(first_group, s0, s1) * scale, sel_ref[:, c * width:(c + 1) * width] > 0.5, values)

    _paged_chunks(pt_ref, seq, layer, (k_hbm, v_hbm), (k_buf, v_buf), sems, ('rows', 'rows'), chunk, n_chunks,
                  compute)

    kn = kn_ref[...]
    vn = vn_ref[...]
    k_row = jnp.where(first_group, kn[:, :A_HEAD_DIM], kn[:, A_HEAD_DIM:])
    v_row = jnp.where(first_group, vn[:, :A_HEAD_DIM], vn[:, A_HEAD_DIM:])
    sm = jnp.sum(q.astype(f32) * k_row, axis=1, keepdims=True) * scale
    online(sm, seln_ref[:, 0:1] > 0.5, lambda p: p * v_row)
    oa_ref[...] = state[2] / state[1]


def _sweep2(qa, kn, vn, sel, sel_new, cache_k, cache_v, page_table, layer, *, chunk):
    S, n_pages = page_table.shape
    rows = chunk * cache_k.shape[2]
    grid_spec = pltpu.PrefetchScalarGridSpec(
        num_scalar_prefetch=1, grid=(S,),
        in_specs=[_seq_spec(a) for a in (qa, kn, vn, sel, sel_new)] + [pl.BlockSpec(memory_space=pl.ANY)] * 2,
        out_specs=pl.BlockSpec((None, _ROW_PAD, A_HEAD_DIM), lambda b, pt: (b, 0, 0)),
        scratch_shapes=[pltpu.VMEM((2, rows, A_HEAD_DIM), f32), pltpu.VMEM((2, rows, A_HEAD_DIM), f32),
                        pltpu.SemaphoreType.DMA((2, 2))])
    return pl.pallas_call(
        functools.partial(_sweep2_kernel, layer=layer, chunk=chunk, n_chunks=n_pages // chunk),
        grid_spec=grid_spec,
        out_shape=jax.ShapeDtypeStruct((S, _ROW_PAD, A_HEAD_DIM), f32),
        compiler_params=_params(("arbitrary",)),
        name="decode_dsa",
    )(page_table, qa, kn, vn, sel, sel_new, cache_k, cache_v)


def _rope_tables(pos):
    def cs(half):
        inv = ROPE_THETA ** (-jnp.arange(half, dtype=f32) / half)
        ang = pos.astype(f32)[:, None] * inv[None, :]
        return jnp.cos(ang), jnp.sin(ang)

    c, s = cs(A_HEAD_DIM // 2)
    cos_a = jnp.concatenate([c, c], axis=1)
    sin_a = jnp.concatenate([-s, s], axis=1)
    c, s = cs(IDX_DIM // 2)
    z = jnp.zeros_like(s)
    cos_i = jnp.concatenate([c, c, c, c], axis=1)
    sin_lo = jnp.concatenate([-s, z, -s, z], axis=1)
    sin_hi = jnp.concatenate([z, s, z, s], axis=1)
    return cos_a, sin_a, cos_i, sin_lo, sin_hi


def _layer_weights(l, w_in, attn_norm, c_q_norm, c_w_uq, c_kv_norm, c_w_uk, c_w_uv, b_v_norm, b_w_s, b_bias,
                   w_br_a, w_br_b, w_br_c, w_out, ffn_norm, ffn_w_up, ffn_conv_w, ffn_conv_b, ffn_w_down):
    w = w_in[l]
    o_ki = A_WIDTH + 2 * A_KV_WIDTH + IDX_WIDTH
    o_wi = o_ki + IDX_DIM
    o_u = o_wi + IDX_HEADS
    o_cq = o_u + 2 * B_WIDTH
    o_kr = o_cq + Q_LORA + KV_LORA
    o_gl = o_kr + C_ROPE
    w_all = jnp.concatenate(
        [w[:, :o_ki], w[:, o_ki:o_wi], w[:, o_kr:o_gl],
         jnp.pad(w[:, o_wi:o_u], ((0, 0), (0, LANES - IDX_HEADS))), w[:, o_u:o_cq], w[:, o_cq:o_kr]], axis=1)
    uq = c_w_uq[l]
    return {
        'attn_norm': attn_norm[l][None], 'w_all': w_all.astype(bf16), 'w_gate': w[:, o_gl:].astype(bf16),
        'c_q_norm': c_q_norm[l][None], 'c_kv_norm': c_kv_norm[l][None], 'b_v_norm': b_v_norm[l][None],
        'c_w_uq': jnp.concatenate([uq[:, :, :C_NOPE].reshape(Q_LORA, C_HEADS * C_NOPE),
                                   uq[:, :, C_NOPE:].reshape(Q_LORA, C_HEADS * C_ROPE)], axis=1).astype(bf16),
        'c_w_uk': jnp.transpose(c_w_uk[l], (1, 2, 0)).astype(bf16),
        'c_w_uv': jnp.transpose(c_w_uv[l], (1, 0, 2)).astype(bf16),
        'b_w_s': b_w_s[l], 'b_bias_t': b_bias[l].T,
        'b_w_00': jnp.repeat(b_w_s[l][:, 0, 0], B_GROUP_DIM)[None], 'b_bias_0': jnp.repeat(b_bias[l][:, 0], B_GROUP_DIM)[None],
        'w_br_a': w_br_a[l].astype(bf16), 'w_br_b': w_br_b[l].astype(bf16), 'w_br_c': w_br_c[l].astype(bf16),
        'w_out': w_out[l].astype(bf16), 'ffn_norm': ffn_norm[l][None],
        'ffn_w_a': ffn_w_up[l][:, :D_FF].astype(bf16), 'ffn_w_g': ffn_w_up[l][:, D_FF:].astype(bf16),
        'ffn_conv_w': ffn_conv_w[l], 'ffn_conv_b': ffn_conv_b[l][None], 'ffn_w_down': ffn_w_down[l].astype(bf16),
    }


def _pad_rows(a):
    return jnp.pad(a, ((0, 0), (0, _ROW_PAD - a.shape[1]), (0, 0)))


TM_PROJ = 256
TM_MIX = 256
TM_FFN = 256
TQ_DSA = 256
TQ_MLA = 128
CAUSAL_BUCKETS = 8
SWEEP_PAGES = 32


def kernel(x_prompt, x_sample, cache_a_k, cache_a_v, cache_a_idxk, cache_c_latent, cache_c_krope, state_ffn_conv, page_table, attn_norm, w_in, c_q_norm, c_w_uq, c_kv_norm, c_w_uk, c_w_uv, b_v_norm, b_w_s, b_bias, w_br_a, w_br_b, w_br_c, w_out, ffn_norm, ffn_w_up, ffn_conv_w, ffn_conv_b, ffn_w_down, final_norm):
    bsz, seq, _ = x_prompt.shape
    dbsz, dseq, _ = x_sample.shape
    depth = w_in.shape[0]
    assert dseq == 1 and seq % CHUNK == 0
    n_pages = page_table.shape[1]
    past_len = n_pages * PAGE_SIZE
    assert past_len % CHUNK == 0
    topk_p = min(TOPK_MAX, seq // 4)
    topk_s = min(TOPK_MAX, (past_len + dseq) // 4)
    n_buckets = min(CAUSAL_BUCKETS, seq // max(TQ_DSA, TQ_MLA, topk_p))
    chunk = min(SWEEP_PAGES, n_pages)
    assert n_pages % chunk == 0
    tabs_p = _rope_tables(jnp.arange(seq, dtype=jnp.int32))
    tabs_s = _rope_tables(jnp.full((dbsz,), past_len, dtype=jnp.int32))
    fin = final_norm[None]
    cache_k2 = cache_a_k.reshape(cache_a_k.shape[:2] + (PAGE_SIZE * A_KV_HEADS, A_HEAD_DIM))
    cache_v2 = cache_a_v.reshape(cache_a_v.shape[:2] + (PAGE_SIZE * A_KV_HEADS, A_HEAD_DIM))
    cache_idx_t = jnp.swapaxes(cache_a_idxk, 2, 3)
    cache_kro_t = jnp.swapaxes(cache_c_krope, 2, 3)

    xp = x_prompt
    xs = x_sample.reshape(1, dbsz, D_MODEL)
    outs = {k: [] for k in ('pk', 'pv', 'pik', 'pcl', 'pcr', 'pfc', 'sk', 'sv', 'sik', 'scl', 'scr', 'sfc', 'sbv')}
    yp = ys = None
    for l in range(depth):
        lw = _layer_weights(l, w_in, attn_norm, c_q_norm, c_w_uq, c_kv_norm, c_w_uk, c_w_uv, b_v_norm, b_w_s,
                            b_bias, w_br_a, w_br_b, w_br_c, w_out, ffn_norm, ffn_w_up, ffn_conv_w, ffn_conv_b,
                            ffn_w_down)
        final = l == depth - 1

        pr = _project(xp, dict(lw, gate_w=lw['b_w_s'], gate_b=lw['b_bias_t']), tabs_p, tm=TM_PROJ, decode=False)
        oa = _dsa_prompt(pr, tq=TQ_DSA, topk=topk_p, n_buckets=n_buckets)
        ol = _mla_prompt(pr, tq=TQ_MLA, n_buckets=n_buckets)
        x1 = _mixer(xp, oa, pr['ob'], ol, lw, tm=TM_MIX)
        res = _ffn(x1, lw, fin, tm=TM_FFN, final=final)
        xp = res[0]
        if final:
            yp = res[2]
        outs['pk'].append(pr['ka'].reshape(bsz, seq, A_KV_HEADS, A_HEAD_DIM))
        outs['pv'].append(pr['va'].reshape(bsz, seq, A_KV_HEADS, A_HEAD_DIM))
        outs['pik'].append(pr['ki']); outs['pcl'].append(pr['ckv']); outs['pcr'].append(pr['kr'])
        outs['pfc'].append(res[1])

        sr = _project(xs, dict(lw, gate_w=lw['b_w_00'], gate_b=lw['b_bias_0']), tabs_s, tm=dbsz, decode=True)
        seqs = lambda a, r: a.reshape(dbsz, r, a.shape[-1] // r)
        sweep_in = (_pad_rows(seqs(sr['qi'], IDX_HEADS)),
                    _pad_rows(sr['wi'][0, :, :IDX_HEADS, None]),
                    _pad_rows(seqs(sr['qlat'], C_HEADS)), _pad_rows(seqs(sr['qrope'], C_HEADS)),
                    seqs(sr['ki'], 1), seqs(sr['ckv'], 1), seqs(sr['kr'], 1))
        scores, score_new, ol_s = _sweep1(sweep_in, (cache_idx_t, cache_c_latent, cache_kro_t), page_table, l,
                                          chunk=chunk)
        sel = _select(jnp.concatenate([scores[:, 0], score_new[:, 0]], axis=1), topk=topk_s)
        oa_s = _sweep2(_pad_rows(seqs(sr['qa'], A_HEADS)), seqs(sr['ka'], 1), seqs(sr['va'], 1),
                       sel[:, None, :past_len], sel[:, None, past_len:],
                       cache_k2, cache_v2, page_table, l, chunk=chunk)
        oa_s = oa_s[:, :A_HEADS].reshape(1, dbsz, A_WIDTH).astype(bf16)
        ol_s = ol_s[:, :C_HEADS].reshape(1, dbsz, C_HEADS * KV_LORA).astype(bf16)
        x1s = _mixer(xs, oa_s, sr['ob'], ol_s, lw, tm=dbsz)
        hist = state_ffn_conv[l]
        res_s = _ffn(x1s, lw, fin, tm=dbsz, final=final,
                     hist=(hist[None, :, 0], hist[None, :, 1]))
        xs = res_s[0]
        if final:
            ys = res_s[2]
        outs['sk'].append(sr['ka'].reshape(dbsz, dseq, A_KV_HEADS, A_HEAD_DIM))
        outs['sv'].append(sr['va'].reshape(dbsz, dseq, A_KV_HEADS, A_HEAD_DIM))
        outs['sik'].append(sr['ki'].reshape(dbsz, dseq, IDX_DIM))
        outs['scl'].append(sr['ckv'].reshape(dbsz, dseq, KV_LORA))
        outs['scr'].append(sr['kr'].reshape(dbsz, dseq, C_ROPE))
        outs['sfc'].append(jnp.stack([hist[:, 1], res_s[1][0]], axis=1))
        outs['sbv'].append(sr['vn'].reshape(dbsz, dseq, B_WIDTH))

    st = {k: jnp.stack(v) for k, v in outs.items()}
    return (yp, ys.reshape(dbsz, dseq, D_MODEL), st['pk'], st['pv'], st['pik'], st['pcl'], st['pcr'], st['pfc'],
            st['sk'], st['sv'], st['sik'], st['scl'], st['scr'], st['sfc'], st['sbv'])
```

```python
import functools

import jax
import jax.numpy as jnp
from jax import lax
from jax.experimental import pallas as pl
from jax.experimental.pallas import tpu as pltpu

D_MODEL = 1024
PAGE_SIZE = 128
A_HEADS = 4
A_KV_HEADS = 2
A_HEAD_DIM = 128
A_WIDTH = A_HEADS * A_HEAD_DIM
A_KV_WIDTH = A_KV_HEADS * A_HEAD_DIM
IDX_HEADS = 8
IDX_DIM = 64
IDX_WIDTH = IDX_HEADS * IDX_DIM
TOPK_MAX = 256
CHUNK = 128
B_GROUPS = 4
B_GROUP_DIM = 128
B_WIDTH = B_GROUPS * B_GROUP_DIM
C_HEADS = 4
Q_LORA = 256
KV_LORA = 256
C_NOPE = 128
C_ROPE = 64
C_VDIM = 128
C_WIDTH = C_HEADS * C_VDIM
N_BRANCH = 3
D_FF = 2816
CONV_W = 3
ROPE_THETA = 10000.0
EPS = 1e-6

LANES = 128
SUBLANES = 8
VMEM_LIMIT = 56 * 1024 * 1024

_O_QA = 0
_O_KA = _O_QA + A_WIDTH
_O_VA = _O_KA + A_KV_WIDTH
_O_QI = _O_VA + A_KV_WIDTH
_O_KIKR = _O_QI + IDX_WIDTH
_O_WI = _O_KIKR + LANES
_O_U = _O_WI + LANES
_O_V = _O_U + B_WIDTH
_O_CQ = _O_V + B_WIDTH
_O_CKV = _O_CQ + Q_LORA
_W_ALL = _O_CKV + KV_LORA

_NT = (((1,), (1,)), ((), ()))
_INT_MIN = -2147483648
_NEG_INF_KEY = -2139095041
_ROW_PAD = 16
_SEARCH_GROUP_ROWS = 64
_MLA_HEAD_STACK = 2

f32 = jnp.float32
bf16 = jnp.bfloat16


def _rms(x, g):
    return x * lax.rsqrt(jnp.mean(x * x, axis=-1, keepdims=True) + EPS) * g


def _rope_full(x, cos, sin):
    parts = []
    for k in range(x.shape[1] // LANES):
        xs = x[:, k * LANES:(k + 1) * LANES]
        parts.append(xs * cos + pltpu.roll(xs, LANES // 2, 1) * sin)
    return parts[0] if len(parts) == 1 else jnp.concatenate(parts, axis=1)


def _rope_half(x, cos, sin_lo, sin_hi):
    parts = []
    for k in range(x.shape[1] // LANES):
        xs = x[:, k * LANES:(k + 1) * LANES]
        parts.append(xs * cos + pltpu.roll(xs, LANES - IDX_DIM // 2, 1) * sin_lo
                     + pltpu.roll(xs, IDX_DIM // 2, 1) * sin_hi)
    return parts[0] if len(parts) == 1 else jnp.concatenate(parts, axis=1)


def _order_key(score):
    bits = lax.bitcast_convert_type(score, jnp.int32)
    return bits ^ ((bits >> 31) & jnp.int32(0x7FFFFFFF))


def _count(mask):
    return jnp.sum(jnp.where(mask, 1.0, 0.0), axis=1, keepdims=True)


def _kth_largest_key(key, k):
    rows = key.shape[0]
    group = min(rows, _SEARCH_GROUP_ROWS)
    parts = [key[r:r + group] for r in range(0, rows, group)]

    def body(i, tbs):
        bit = jnp.left_shift(jnp.int32(1), 31 - i)
        out = []
        for part, tb in zip(parts, tbs):
            cand_b = tb | bit
            cnt = _count(part >= (cand_b ^ jnp.int32(_INT_MIN)))
            out.append(jnp.where(cnt >= k, cand_b, tb))
        return tuple(out)

    tbs = lax.fori_loop(0, 32, body, tuple(jnp.zeros((group, 1), jnp.int32) for _ in parts))
    tb = tbs[0] if len(tbs) == 1 else jnp.concatenate(tbs, axis=0)
    return tb ^ jnp.int32(_INT_MIN)


def _topk_membership(key, k, sel_ref):
    rows, n = key.shape
    thr = _kth_largest_key(key, k)
    gt = key > thr
    eq = key == thr
    need = k - _count(gt)
    sel_ref[...] = jnp.where(gt | eq, 1.0, 0.0)
    tie = (thr > _NEG_INF_KEY) & (_count(eq) > need)

    @pl.when(jnp.max(jnp.where(tie, 1.0, 0.0)) > 0.0)
    def _():
        r = lax.broadcasted_iota(jnp.int32, (LANES, LANES), 0)
        c = lax.broadcasted_iota(jnp.int32, (LANES, LANES), 1)
        upper = jnp.where(r < c, 1.0, 0.0).astype(bf16)
        run = jnp.zeros((rows, 1), f32)
        for j in range(n // LANES):
            sl = slice(j * LANES, (j + 1) * LANES)
            eq_j = jnp.where(eq[:, sl], 1.0, 0.0)
            before = jnp.dot(eq_j.astype(bf16), upper, preferred_element_type=f32) + run
            keep = gt[:, sl] | (eq[:, sl] & (before < need))
            sel_ref[:, sl] = jnp.where(keep, 1.0, 0.0)
            run = run + jnp.sum(eq_j, axis=1, keepdims=True)


def _softmax_rows(s):
    p = jnp.exp(s - jnp.max(s, axis=1, keepdims=True))
    return p / jnp.sum(p, axis=1, keepdims=True)


def _const_spec(shape):
    nd = len(shape)
    return pl.BlockSpec(shape, lambda *_: (0,) * nd, pipeline_mode=pl.Buffered(1))


def _params(sem):
    return pltpu.CompilerParams(dimension_semantics=sem, vmem_limit_bytes=VMEM_LIMIT)


def _causal_buckets(j, tq, L, n_buckets, body):
    step = L // n_buckets
    per = step // tq
    for bk in range(n_buckets):
        @pl.when((j >= bk * per) & (j < (bk + 1) * per))
        def _(bk=bk):
            body((bk + 1) * step)


_PROJ_OUTS = (('qa', A_WIDTH, bf16), ('ka', A_KV_WIDTH, f32), ('va', A_KV_WIDTH, f32), ('qi', IDX_WIDTH, bf16),
              ('ki', IDX_DIM, f32), ('kr', C_ROPE, f32), ('wi', LANES, f32), ('ob', B_WIDTH, bf16),
              ('ckv', KV_LORA, f32), ('qlat', C_HEADS * KV_LORA, bf16), ('qrope', C_HEADS * C_ROPE, bf16))
_PROJ_OUTS_PROMPT = (('ka_b', A_KV_WIDTH, bf16), ('va_b', A_KV_WIDTH, bf16), ('ki_b', IDX_DIM, bf16),
                     ('ckv_b', KV_LORA, bf16), ('kr_b', C_ROPE, bf16))
_PROJ_OUTS_DECODE = (('vn', B_WIDTH, f32),)


def _proj_kernel(x_ref, gain_ref, w_ref, cqn_ref, wuq_ref, wuk_ref, ckvn_ref, bvn_ref, ws_ref, bias_ref,
                 cosa_ref, sina_ref, cosi_ref, silo_ref, sihi_ref, *out_refs, decode):
    names = [n for n, _, _ in _PROJ_OUTS + (_PROJ_OUTS_DECODE if decode else _PROJ_OUTS_PROMPT)]
    o = dict(zip(names, out_refs))
    tm = x_ref.shape[0]
    h = _rms(x_ref[...], gain_ref[...]).astype(bf16)
    z_all = jnp.dot(h, w_ref[...], preferred_element_type=f32)

    def z(lo, hi):
        return z_all[:, lo:hi]

    cosa, sina = cosa_ref[...], sina_ref[...]
    cosi, silo, sihi = cosi_ref[...], silo_ref[...], sihi_ref[...]

    o['qa'][...] = _rope_full(z(_O_QA, _O_KA), cosa, sina).astype(bf16)
    ka = _rope_full(z(_O_KA, _O_VA), cosa, sina)
    va = z(_O_VA, _O_QI)
    o['ka'][...] = ka
    o['va'][...] = va
    o['qi'][...] = _rope_half(z(_O_QI, _O_KIKR), cosi, silo, sihi).astype(bf16)
    kikr = _rope_half(z(_O_KIKR, _O_WI), cosi, silo, sihi)
    o['ki'][...] = kikr[:, :IDX_DIM]
    o['kr'][...] = kikr[:, IDX_DIM:]
    o['wi'][...] = z(_O_WI, _O_U) * IDX_HEADS ** -0.5
    ckv = _rms(z(_O_CKV, _W_ALL), ckvn_ref[...])
    o['ckv'][...] = ckv
    if not decode:
        o['ka_b'][...] = ka.astype(bf16)
        o['va_b'][...] = va.astype(bf16)
        kikr_b = kikr.astype(bf16)
        o['ki_b'][...] = kikr_b[:, :IDX_DIM]
        o['kr_b'][...] = kikr_b[:, IDX_DIM:]
        o['ckv_b'][...] = ckv.astype(bf16)

    u = jax.nn.gelu(z(_O_U, _O_V))
    v = _rms(jax.nn.gelu(z(_O_V, _O_CQ)), bvn_ref[...])
    if decode:
        o['vn'][...] = v
        o['ob'][...] = (u * (ws_ref[...] * v + bias_ref[...])).astype(bf16)
    else:
        r = lax.broadcasted_iota(jnp.int32, (CHUNK, CHUNK), 0)
        c = lax.broadcasted_iota(jnp.int32, (CHUNK, CHUNK), 1)
        vb = v.astype(bf16)
        bias = bias_ref[...]
        for g in range(B_GROUPS):
            w_g = jnp.where(r >= c, ws_ref[g], 0.0).astype(bf16)
            cols = slice(g * B_GROUP_DIM, (g + 1) * B_GROUP_DIM)
            for k in range(tm // CHUNK):
                rows = slice(k * CHUNK, (k + 1) * CHUNK)
                mixed = jnp.dot(w_g, vb[rows, cols], preferred_element_type=f32) + bias[:, g:g + 1]
                o['ob'][rows, cols] = (u[rows, cols] * mixed).astype(bf16)

    cq = _rms(z(_O_CQ, _O_CKV), cqn_ref[...]).astype(bf16)
    qc = jnp.dot(cq, wuq_ref[...], preferred_element_type=f32)
    for hd in range(C_HEADS):
        q_nope = qc[:, hd * C_NOPE:(hd + 1) * C_NOPE].astype(bf16)
        o['qlat'][:, hd * KV_LORA:(hd + 1) * KV_LORA] = jnp.dot(
            q_nope, wuk_ref[hd], preferred_element_type=f32).astype(bf16)
    o['qrope'][...] = _rope_half(qc[:, C_HEADS * C_NOPE:], cosi, silo, sihi).astype(bf16)


def _project(x, lw, tabs, *, tm, decode):
    B, L, _ = x.shape
    outs = _PROJ_OUTS + (_PROJ_OUTS_DECODE if decode else _PROJ_OUTS_PROMPT)

    def tok(width):
        return pl.BlockSpec((None, tm, width), lambda b, i: (b, i, 0))

    weights = [lw['attn_norm'], lw['w_all'], lw['c_q_norm'], lw['c_w_uq'], lw['c_w_uk'], lw['c_kv_norm'],
               lw['b_v_norm'], lw['gate_w'], lw['gate_b']]
    in_specs = ([tok(D_MODEL)] + [_const_spec(w.shape) for w in weights]
                + [pl.BlockSpec((tm, LANES), lambda b, i: (i, 0)) for _ in range(5)])
    res = pl.pallas_call(
        functools.partial(_proj_kernel, decode=decode),
        grid=(B, L // tm), in_specs=in_specs,
        out_specs=[tok(w) for _, w, _ in outs],
        out_shape=[jax.ShapeDtypeStruct((B, L, w), dt) for _, w, dt in outs],
        compiler_params=_params(("parallel", "parallel")),
        name="proj_decode" if decode else "proj_prompt",
    )(x, *weights, *tabs)
    return dict(zip([n for n, _, _ in outs], res))


def _dsa_kernel(qi_ref, wi_ref, ki_ref, qa_ref, ka_ref, va_ref, oa_ref, sel_ref, *, topk, first_block):
    tq, ext = sel_ref.shape
    j = first_block + pl.program_id(1)
    rep = A_HEADS // A_KV_HEADS
    ki = ki_ref[...]
    wi = wi_ref[...] * IDX_DIM ** -0.5
    score = jnp.zeros((tq, ext), f32)
    for hd in range(IDX_HEADS):
        s = lax.dot_general(qi_ref[:, hd * IDX_DIM:(hd + 1) * IDX_DIM], ki, _NT, preferred_element_type=f32)
        score = score + wi[:, hd:hd + 1] * jnp.maximum(s, 0.0)
    qpos = j * tq + lax.broadcasted_iota(jnp.int32, (tq, ext), 0)
    kpos = lax.broadcasted_iota(jnp.int32, (tq, ext), 1)
    key = _order_key(jnp.where(kpos <= qpos, score, -jnp.inf))
    _topk_membership(key, topk, sel_ref)
    sel = (sel_ref[...] > 0.5) & (key > _NEG_INF_KEY)
    sel2 = jnp.concatenate([sel] * rep, axis=0)
    for g in range(A_KV_HEADS):
        kv = slice(g * A_HEAD_DIM, (g + 1) * A_HEAD_DIM)
        q = jnp.concatenate([qa_ref[:, (g * rep + r) * A_HEAD_DIM:(g * rep + r + 1) * A_HEAD_DIM]
                             for r in range(rep)], axis=0)
        s = lax.dot_general(q, ka_ref[:, kv], _NT, preferred_element_type=f32) * A_HEAD_DIM ** -0.5
        p = _softmax_rows(jnp.where(sel2, s, -jnp.inf))
        out = jnp.dot(p.astype(bf16), va_ref[:, kv], preferred_element_type=f32).astype(bf16)
        for r in range(rep):
            hd = g * rep + r
            oa_ref[:, hd * A_HEAD_DIM:(hd + 1) * A_HEAD_DIM] = out[r * tq:(r + 1) * tq]


def _dsa_prompt(pr, *, tq, topk, n_buckets):
    B, L, _ = pr['qa'].shape
    step = L // n_buckets
    per = step // tq
    pieces = []
    for bk in range(n_buckets):
        ext = (bk + 1) * step

        def qspec(width, bk=bk):
            return pl.BlockSpec((None, tq, width), lambda b, j: (b, bk * per + j, 0))

        def kspec(width, ext=ext):
            return pl.BlockSpec((None, ext, width), lambda b, j: (b, 0, 0))

        pieces.append(pl.pallas_call(
            functools.partial(_dsa_kernel, topk=topk, first_block=bk * per),
            grid=(B, per),
            in_specs=[qspec(IDX_WIDTH), qspec(LANES), kspec(IDX_DIM), qspec(A_WIDTH), kspec(A_KV_WIDTH),
                      kspec(A_KV_WIDTH)],
            out_specs=pl.BlockSpec((None, tq, A_WIDTH), lambda b, j: (b, j, 0)),
            out_shape=jax.ShapeDtypeStruct((B, step, A_WIDTH), bf16),
            scratch_shapes=[pltpu.VMEM((tq, ext), f32)],
            compiler_params=_params(("parallel", "parallel")),
            name=f"dsa_prompt_{ext}",
        )(pr['qi'], pr['wi'], pr['ki_b'], pr['qa'], pr['ka_b'], pr['va_b']))
    return jnp.concatenate(pieces, axis=1)


def _mla_kernel(ql_ref, qr_ref, c_ref, kr_ref, ol_ref, *, n_buckets):
    tq = ql_ref.shape[0]
    L = c_ref.shape[0]
    j = pl.program_id(1)

    def body(ext):
        c = c_ref[0:ext, :]
        kr = kr_ref[0:ext, :]
        row = lax.broadcasted_iota(jnp.int32, (_MLA_HEAD_STACK * tq, ext), 0)
        qpos = j * tq + (row & (tq - 1))
        kpos = lax.broadcasted_iota(jnp.int32, (_MLA_HEAD_STACK * tq, ext), 1)
        causal = kpos <= qpos
        for h0 in range(0, C_HEADS, _MLA_HEAD_STACK):
            heads = range(h0, h0 + _MLA_HEAD_STACK)
            ql = jnp.concatenate([ql_ref[:, hd * KV_LORA:(hd + 1) * KV_LORA] for hd in heads], axis=0)
            qr = jnp.concatenate([qr_ref[:, hd * C_ROPE:(hd + 1) * C_ROPE] for hd in heads], axis=0)
            s = (lax.dot_general(ql, c, _NT, preferred_element_type=f32)
                 + lax.dot_general(qr, kr, _NT, preferred_element_type=f32))
            p = _softmax_rows(jnp.where(causal, s * (C_NOPE + C_ROPE) ** -0.5, -jnp.inf))
            out = jnp.dot(p.astype(bf16), c, preferred_element_type=f32).astype(bf16)
            for n, hd in enumerate(heads):
                ol_ref[:, hd * KV_LORA:(hd + 1) * KV_LORA] = out[n * tq:(n + 1) * tq]

    _causal_buckets(j, tq, L, n_buckets, body)


def _mla_prompt(pr, *, tq, n_buckets):
    B, L, _ = pr['qlat'].shape
    assert tq & (tq - 1) == 0

    def qspec(width):
        return pl.BlockSpec((None, tq, width), lambda b, j: (b, j, 0))

    def kspec(width):
        return pl.BlockSpec((None, L, width), lambda b, j: (b, 0, 0))

    return pl.pallas_call(
        functools.partial(_mla_kernel, n_buckets=n_buckets),
        grid=(B, L // tq),
        in_specs=[qspec(C_HEADS * KV_LORA), qspec(C_HEADS * C_ROPE), kspec(KV_LORA), kspec(C_ROPE)],
        out_specs=qspec(C_HEADS * KV_LORA),
        out_shape=jax.ShapeDtypeStruct((B, L, C_HEADS * KV_LORA), bf16),
        compiler_params=_params(("parallel", "parallel")),
        name="mla_prompt",
    )(pr['qlat'], pr['qrope'], pr['ckv_b'], pr['kr_b'])


def _mix_kernel(x_ref, oa_ref, ob_ref, ol_ref, gain_ref, wg_ref, wa_ref, wb_ref, wc_ref, wuv_ref, wo_ref, x1_ref):
    x = x_ref[...]
    h = _rms(x, gain_ref[...]).astype(bf16)
    oc = jnp.concatenate(
        [jnp.dot(ol_ref[:, hd * KV_LORA:(hd + 1) * KV_LORA], wuv_ref[hd], preferred_element_type=f32)
         for hd in range(C_HEADS)], axis=1).astype(bf16)
    merged = None
    for k, (o, w_ref) in enumerate(((oa_ref[...], wa_ref), (ob_ref[...], wb_ref), (oc, wc_ref))):
        gate = jax.nn.sigmoid(jnp.dot(h, wg_ref[:, k * D_MODEL:(k + 1) * D_MODEL], preferred_element_type=f32))
        term = gate * jnp.dot(o, w_ref[...], preferred_element_type=f32)
        merged = term if merged is None else merged + term
    x1_ref[...] = x + jnp.dot(merged.astype(bf16), wo_ref[...], preferred_element_type=f32)


def _mixer(x, oa, ob, ol, lw, *, tm):
    B, L, _ = x.shape

    def tok(width):
        return pl.BlockSpec((None, tm, width), lambda b, i: (b, i, 0))

    weights = [lw['attn_norm'], lw['w_gate'], lw['w_br_a'], lw['w_br_b'], lw['w_br_c'], lw['c_w_uv'], lw['w_out']]
    return pl.pallas_call(
        _mix_kernel,
        grid=(B, L // tm),
        in_specs=[tok(D_MODEL), tok(A_WIDTH), tok(B_WIDTH), tok(C_HEADS * KV_LORA)]
        + [_const_spec(w.shape) for w in weights],
        out_specs=tok(D_MODEL),
        out_shape=jax.ShapeDtypeStruct((B, L, D_MODEL), f32),
        compiler_params=_params(("parallel", "parallel")),
        name="mixer",
    )(x, oa, ob, ol, *weights)


def _ffn_kernel(*refs, decode, final):
    if decode:
        x1_ref, gain_ref, wa_ref, wg_ref, cw_ref, cb_ref, wd_ref, fin_ref, h0_ref, h1_ref = refs[:10]
        outs = refs[10:]
    else:
        x1_ref, gain_ref, wa_ref, wg_ref, cw_ref, cb_ref, wd_ref, fin_ref = refs[:8]
        outs = refs[8:]
    x2_ref, hist_ref = outs[0], outs[1]
    tm = x1_ref.shape[0]
    x1 = x1_ref[...]
    h = _rms(x1, gain_ref[...]).astype(bf16)
    a = jnp.dot(h, wa_ref[...], preferred_element_type=f32)
    g = jnp.dot(h, wg_ref[...], preferred_element_type=f32)
    cw = cw_ref[...]
    if decode:
        hist_ref[...] = g
        conv = g * cw[2:3] + cb_ref[...] + h0_ref[...] * cw[0:1] + h1_ref[...] * cw[1:2]
    else:
        gbuf = outs[-1]

        @pl.when(pl.program_id(1) == 0)
        def _():
            gbuf[0:SUBLANES, :] = jnp.zeros((SUBLANES, D_FF), f32)

        gbuf[SUBLANES:SUBLANES + tm, :] = g
        conv = g * cw[2:3] + cb_ref[...]
        for k in range(CONV_W - 1):
            off = SUBLANES - (CONV_W - 1) + k
            conv = conv + gbuf[off:off + tm, :] * cw[k:k + 1]
        hist_ref[...] = g[tm - (CONV_W - 1):, :]
        gbuf[0:SUBLANES, :] = g[tm - SUBLANES:, :]
    y = jnp.dot((jax.nn.silu(conv) * a).astype(bf16), wd_ref[...], preferred_element_type=f32)
    x2 = x1 + y
    x2_ref[...] = x2
    if final:
        outs[2][...] = _rms(x2, fin_ref[...])


def _ffn(x1, lw, final_norm, *, tm, final, hist=None):
    B, L, _ = x1.shape
    decode = hist is not None

    def tok(width):
        return pl.BlockSpec((None, tm, width), lambda b, i: (b, i, 0))

    weights = [lw['ffn_norm'], lw['ffn_w_a'], lw['ffn_w_g'], lw['ffn_conv_w'], lw['ffn_conv_b'], lw['ffn_w_down'],
               final_norm]
    in_specs = [tok(D_MODEL)] + [_const_spec(w.shape) for w in weights]
    args = [x1, *weights]
    out_specs = [tok(D_MODEL)]
    out_shape = [jax.ShapeDtypeStruct((B, L, D_MODEL), f32)]
    scratch = []
    if decode:
        in_specs += [tok(D_FF), tok(D_FF)]
        args += list(hist)
        out_specs.append(tok(D_FF))
        out_shape.append(jax.ShapeDtypeStruct((B, L, D_FF), f32))
    else:
        out_specs.append(pl.BlockSpec((None, CONV_W - 1, D_FF), lambda b, i: (b, 0, 0)))
        out_shape.append(jax.ShapeDtypeStruct((B, CONV_W - 1, D_FF), f32))
        scratch.append(pltpu.VMEM((tm + SUBLANES, D_FF), f32))
    if final:
        out_specs.append(tok(D_MODEL))
        out_shape.append(jax.ShapeDtypeStruct((B, L, D_MODEL), f32))
    return pl.pallas_call(
        functools.partial(_ffn_kernel, decode=decode, final=final),
        grid=(B, L // tm), in_specs=in_specs, out_specs=out_specs, out_shape=out_shape,
        scratch_shapes=scratch,
        compiler_params=_params(("parallel", "arbitrary")),
        name="ffn_decode" if decode else "ffn_prompt",
    )(*args)


def _select_kernel(sc_ref, sel_ref, *, topk):
    _topk_membership(_order_key(sc_ref[...]), topk, sel_ref)


def _select(scores, *, topk):
    return pl.pallas_call(
        functools.partial(_select_kernel, topk=topk),
        out_shape=jax.ShapeDtypeStruct(scores.shape, f32),
        compiler_params=pltpu.CompilerParams(vmem_limit_bytes=VMEM_LIMIT),
        name="decode_select",
    )(scores)


def _page_copies(pt_ref, seq, first_page, i, layer, srcs, bufs, sems, slot, layouts):
    page = pt_ref[seq, first_page + i]
    out = []
    for k, (src, buf, layout) in enumerate(zip(srcs, bufs, layouts)):
        rows = src.shape[2]
        if layout == 'rows':
            dst = buf.at[slot, pl.ds(pl.multiple_of(i * rows, rows), rows), :]
        else:
            dst = buf.at[slot, :, pl.ds(pl.multiple_of(i * PAGE_SIZE, PAGE_SIZE), PAGE_SIZE)]
        out.append(pltpu.make_async_copy(src.at[layer, page], dst, sems.at[k, slot]))
    return out


def _paged_chunks(pt_ref, seq, n_seq, layer, srcs, bufs, sems, layouts, chunk, n_chunks, compute):
    def each(sq, c, slot, act):
        def body(i, carry):
            for cp in _page_copies(pt_ref, sq, c * chunk, i, layer, srcs, bufs, sems, slot, layouts):
                act(cp)
            return carry
        lax.fori_loop(0, chunk, body, 0)

    def start(cp):
        cp.start()

    def wait(cp):
        cp.wait()

    first_slot = (seq * n_chunks) % 2

    @pl.when(seq == 0)
    def _():
        each(seq, 0, first_slot, start)

    for c in range(n_chunks):
        slot = (first_slot + c) % 2
        if c + 1 < n_chunks:
            each(seq, c + 1, 1 - slot, start)
        else:
            @pl.when(seq + 1 < n_seq)
            def _(slot=slot):
                each(seq + 1, 0, 1 - slot, start)
        each(seq, c, slot, wait)
        compute(c, slot)


def _sweep1_kernel(pt_ref, qi_ref, wi_ref, ql_ref, qr_ref, kin_ref, cn_ref, krn_ref, idx_hbm, lat_hbm, kro_hbm,
                   sc_ref, scn_ref, ol_ref, idx_buf, lat_buf, kro_buf, sems, *, layer, chunk, n_chunks):
    seq = pl.program_id(0)
    width = chunk * PAGE_SIZE
    scale_c = (C_NOPE + C_ROPE) ** -0.5
    qi, ql, qr = qi_ref[...], ql_ref[...], qr_ref[...]
    wi = wi_ref[...] * IDX_DIM ** -0.5
    state = [jnp.full((_ROW_PAD, 1), -jnp.inf, f32), jnp.zeros((_ROW_PAD, 1), f32),
             jnp.zeros((_ROW_PAD, KV_LORA), f32)]

    def online(sm, values):
        m_old, l_old, acc = state
        m_new = jnp.maximum(m_old, jnp.max(sm, axis=1, keepdims=True))
        alpha = jnp.exp(m_old - m_new)
        p = jnp.exp(sm - m_new)
        state[:] = [m_new, alpha * l_old + jnp.sum(p, axis=1, keepdims=True), alpha * acc + values(p)]

    def compute(c, slot):
        si = jnp.dot(qi, idx_buf[slot].astype(bf16), preferred_element_type=f32)
        sc_ref[:, c * width:(c + 1) * width] = jnp.sum(wi * jnp.maximum(si, 0.0), axis=0, keepdims=True)
        lat = lat_buf[slot].astype(bf16)
        sm = (lax.dot_general(ql, lat, _NT, preferred_element_type=f32)
              + jnp.dot(qr, kro_buf[slot].astype(bf16), preferred_element_type=f32)) * scale_c
        online(sm, lambda p: jnp.dot(p.astype(bf16), lat, preferred_element_type=f32))

    _paged_chunks(pt_ref, seq, pl.num_programs(0), layer, (idx_hbm, lat_hbm, kro_hbm),
                  (idx_buf, lat_buf, kro_buf), sems,
                  ('lanes', 'rows', 'lanes'), chunk, n_chunks, compute)

    s_new = jnp.sum(qi.astype(f32) * kin_ref[...], axis=1, keepdims=True)
    sc_new = jnp.sum(wi * jnp.maximum(s_new, 0.0), axis=0, keepdims=True)
    lane = lax.broadcasted_iota(jnp.int32, scn_ref.shape, 1)
    scn_ref[...] = jnp.where(lane == 0, sc_new, -jnp.inf)
    cn = cn_ref[...]
    sm = (jnp.sum(ql.astype(f32) * cn, axis=1, keepdims=True)
          + jnp.sum(qr.astype(f32) * krn_ref[...], axis=1, keepdims=True)) * scale_c
    online(sm, lambda p: p * cn)
    ol_ref[...] = state[2] / state[1]


def _seq_spec(a):
    nd = a.ndim - 1
    return pl.BlockSpec((None,) + a.shape[1:], lambda b, pt: (b,) + (0,) * nd)


def _sweep1(sr, caches, page_table, layer, *, chunk):
    S, n_pages = page_table.shape
    past_len = n_pages * PAGE_SIZE
    width = chunk * PAGE_SIZE
    grid_spec = pltpu.PrefetchScalarGridSpec(
        num_scalar_prefetch=1, grid=(S,),
        in_specs=[_seq_spec(a) for a in sr] + [pl.BlockSpec(memory_space=pl.ANY)] * len(caches),
        out_specs=[pl.BlockSpec((None, 1, past_len), lambda b, pt: (b, 0, 0)),
                   pl.BlockSpec((None, 1, LANES), lambda b, pt: (b, 0, 0)),
                   pl.BlockSpec((None, _ROW_PAD, KV_LORA), lambda b, pt: (b, 0, 0))],
        scratch_shapes=[pltpu.VMEM((2, IDX_DIM, width), f32), pltpu.VMEM((2, width, KV_LORA), f32),
                        pltpu.VMEM((2, C_ROPE, width), f32), pltpu.SemaphoreType.DMA((len(caches), 2))])
    return pl.pallas_call(
        functools.partial(_sweep1_kernel, layer=layer, chunk=chunk, n_chunks=n_pages // chunk),
        grid_spec=grid_spec,
        out_shape=[jax.ShapeDtypeStruct((S, 1, past_len), f32),
                   jax.ShapeDtypeStruct((S, 1, LANES), f32),
                   jax.ShapeDtypeStruct((S, _ROW_PAD, KV_LORA), f32)],
        compiler_params=_params(("arbitrary",)),
        name="decode_indexer_mla",
    )(page_table, *sr, *caches)


def _sweep2_kernel(pt_ref, qa_ref, kn_ref, vn_ref, sel_ref, seln_ref, k_hbm, v_hbm, oa_ref, k_buf, v_buf, sems,
                   *, layer, chunk, n_chunks):
    seq = pl.program_id(0)
    width = chunk * PAGE_SIZE
    rep = A_HEADS // A_KV_HEADS
    row = lax.broadcasted_iota(jnp.int32, (_ROW_PAD, 1), 0)
    first_group = row < rep
    scale = A_HEAD_DIM ** -0.5
    q = qa_ref[...]
    state = [jnp.full((_ROW_PAD, 1), -jnp.inf, f32), jnp.zeros((_ROW_PAD, 1), f32),
             jnp.zeros((_ROW_PAD, A_HEAD_DIM), f32)]

    def online(sm, keep, values):
        m_old, l_old, acc = state
        m_new = jnp.maximum(m_old, jnp.max(jnp.where(keep, sm, -jnp.inf), axis=1, keepdims=True))
        m_use = jnp.where(m_new == -jnp.inf, 0.0, m_new)
        alpha = jnp.exp(m_old - m_use)
        p = jnp.where(keep, jnp.exp(sm - m_use), 0.0)
        state[:] = [m_new, alpha * l_old + jnp.sum(p, axis=1, keepdims=True), alpha * acc + values(p)]

    def head_rows(buf, slot, g):
        return buf[slot, pl.ds(g, width, stride=A_KV_HEADS), :].astype(bf16)

    def compute(c, slot):
        s0 = lax.dot_general(q, head_rows(k_buf, slot, 0), _NT, preferred_element_type=f32)
        s1 = lax.dot_general(q, head_rows(k_buf, slot, 1), _NT, preferred_element_type=f32)

        def values(p):
            pb = p.astype(bf16)
            return jnp.where(first_group,
                             jnp.dot(pb, head_rows(v_buf, slot, 0), preferred_element_type=f32),
                             jnp.dot(pb, head_rows(v_buf, slot, 1), preferred_element_type=f32))

        online(jnp.where(first_group, s0, s1) * scale, sel_ref[:, c * width:(c + 1) * width] > 0.5, values)

    _paged_chunks(pt_ref, seq, pl.num_programs(0), layer, (k_hbm, v_hbm), (k_buf, v_buf), sems,
                  ('rows', 'rows'), chunk, n_chunks, compute)

    kn = kn_ref[...]
    vn = vn_ref[...]
    k_row = jnp.where(first_group, kn[:, :A_HEAD_DIM], kn[:, A_HEAD_DIM:])
    v_row = jnp.where(first_group, vn[:, :A_HEAD_DIM], vn[:, A_HEAD_DIM:])
    sm = jnp.sum(q.astype(f32) * k_row, axis=1, keepdims=True) * scale
    online(sm, seln_ref[:, 0:1] > 0.5, lambda p: p * v_row)
    oa_ref[...] = state[2] / state[1]


def _sweep2(qa, kn, vn, sel, sel_new, cache_k, cache_v, page_table, layer, *, chunk):
    S, n_pages = page_table.shape
    rows = chunk * cache_k.shape[2]
    grid_spec = pltpu.PrefetchScalarGridSpec(
        num_scalar_prefetch=1, grid=(S,),
        in_specs=[_seq_spec(a) for a in (qa, kn, vn, sel, sel_new)] + [pl.BlockSpec(memory_space=pl.ANY)] * 2,
        out_specs=pl.BlockSpec((None, _ROW_PAD, A_HEAD_DIM), lambda b, pt: (b, 0, 0)),
        scratch_shapes=[pltpu.VMEM((2, rows, A_HEAD_DIM), f32), pltpu.VMEM((2, rows, A_HEAD_DIM), f32),
                        pltpu.SemaphoreType.DMA((2, 2))])
    return pl.pallas_call(
        functools.partial(_sweep2_kernel, layer=layer, chunk=chunk, n_chunks=n_pages // chunk),
        grid_spec=grid_spec,
        out_shape=jax.ShapeDtypeStruct((S, _ROW_PAD, A_HEAD_DIM), f32),
        compiler_params=_params(("arbitrary",)),
        name="decode_dsa",
    )(page_table, qa, kn, vn, sel, sel_new, cache_k, cache_v)


def _rope_tables(pos):
    def cs(half):
        inv = ROPE_THETA ** (-jnp.arange(half, dtype=f32) / half)
        ang = pos.astype(f32)[:, None] * inv[None, :]
        return jnp.cos(ang), jnp.sin(ang)

    c, s = cs(A_HEAD_DIM // 2)
    cos_a = jnp.concatenate([c, c], axis=1)
    sin_a = jnp.concatenate([-s, s], axis=1)
    c, s = cs(IDX_DIM // 2)
    z = jnp.zeros_like(s)
    cos_i = jnp.concatenate([c, c, c, c], axis=1)
    sin_lo = jnp.concatenate([-s, z, -s, z], axis=1)
    sin_hi = jnp.concatenate([z, s, z, s], axis=1)
    return cos_a, sin_a, cos_i, sin_lo, sin_hi


def _layer_weights(l, w_in, attn_norm, c_q_norm, c_w_uq, c_kv_norm, c_w_uk, c_w_uv, b_v_norm, b_w_s, b_bias,
                   w_br_a, w_br_b, w_br_c, w_out, ffn_norm, ffn_w_up, ffn_conv_w, ffn_conv_b, ffn_w_down):
    w = w_in[l]
    o_ki = A_WIDTH + 2 * A_KV_WIDTH + IDX_WIDTH
    o_wi = o_ki + IDX_DIM
    o_u = o_wi + IDX_HEADS
    o_cq = o_u + 2 * B_WIDTH
    o_kr = o_cq + Q_LORA + KV_LORA
    o_gl = o_kr + C_ROPE
    w_all = jnp.concatenate(
        [w[:, :o_ki], w[:, o_ki:o_wi], w[:, o_kr:o_gl],
         jnp.pad(w[:, o_wi:o_u], ((0, 0), (0, LANES - IDX_HEADS))), w[:, o_u:o_cq], w[:, o_cq:o_kr]], axis=1)
    uq = c_w_uq[l]
    return {
        'attn_norm': attn_norm[l][None], 'w_all': w_all.astype(bf16), 'w_gate': w[:, o_gl:].astype(bf16),
        'c_q_norm': c_q_norm[l][None], 'c_kv_norm': c_kv_norm[l][None], 'b_v_norm': b_v_norm[l][None],
        'c_w_uq': jnp.concatenate([uq[:, :, :C_NOPE].reshape(Q_LORA, C_HEADS * C_NOPE),
                                   uq[:, :, C_NOPE:].reshape(Q_LORA, C_HEADS * C_ROPE)], axis=1).astype(bf16),
        'c_w_uk': jnp.transpose(c_w_uk[l], (1, 2, 0)).astype(bf16),
        'c_w_uv': jnp.transpose(c_w_uv[l], (1, 0, 2)).astype(bf16),
        'b_w_s': b_w_s[l], 'b_bias_t': b_bias[l].T,
        'b_w_00': jnp.repeat(b_w_s[l][:, 0, 0], B_GROUP_DIM)[None], 'b_bias_0': jnp.repeat(b_bias[l][:, 0], B_GROUP_DIM)[None],
        'w_br_a': w_br_a[l].astype(bf16), 'w_br_b': w_br_b[l].astype(bf16), 'w_br_c': w_br_c[l].astype(bf16),
        'w_out': w_out[l].astype(bf16), 'ffn_norm': ffn_norm[l][None],
        'ffn_w_a': ffn_w_up[l][:, :D_FF].astype(bf16), 'ffn_w_g': ffn_w_up[l][:, D_FF:].astype(bf16),
        'ffn_conv_w': ffn_conv_w[l], 'ffn_conv_b': ffn_conv_b[l][None], 'ffn_w_down': ffn_w_down[l].astype(bf16),
    }


def _pad_rows(a):
    return jnp.pad(a, ((0, 0), (0, _ROW_PAD - a.shape[1]), (0, 0)))


TM_PROJ = 512
TM_MIX = 512
TM_FFN = 256
TQ_DSA = 256
TQ_MLA = 128
CAUSAL_BUCKETS = 8
SWEEP_PAGES = 32


def kernel(x_prompt, x_sample, cache_a_k, cache_a_v, cache_a_idxk, cache_c_latent, cache_c_krope, state_ffn_conv, page_table, attn_norm, w_in, c_q_norm, c_w_uq, c_kv_norm, c_w_uk, c_w_uv, b_v_norm, b_w_s, b_bias, w_br_a, w_br_b, w_br_c, w_out, ffn_norm, ffn_w_up, ffn_conv_w, ffn_conv_b, ffn_w_down, final_norm):
    bsz, seq, _ = x_prompt.shape
    dbsz, dseq, _ = x_sample.shape
    depth = w_in.shape[0]
    assert dseq == 1 and seq % CHUNK == 0
    n_pages = page_table.shape[1]
    past_len = n_pages * PAGE_SIZE
    assert past_len % CHUNK == 0
    topk_p = min(TOPK_MAX, seq // 4)
    topk_s = min(TOPK_MAX, (past_len + dseq) // 4)
    n_buckets = min(CAUSAL_BUCKETS, seq // max(TQ_DSA, TQ_MLA, topk_p))
    chunk = min(SWEEP_PAGES, n_pages)
    assert n_pages % chunk == 0
    tabs_p = _rope_tables(jnp.arange(seq, dtype=jnp.int32))
    tabs_s = _rope_tables(jnp.full((dbsz,), past_len, dtype=jnp.int32))
    fin = final_norm[None]
    cache_k2 = cache_a_k.reshape(cache_a_k.shape[:2] + (PAGE_SIZE * A_KV_HEADS, A_HEAD_DIM))
    cache_v2 = cache_a_v.reshape(cache_a_v.shape[:2] + (PAGE_SIZE * A_KV_HEADS, A_HEAD_DIM))
    cache_idx_t = jnp.swapaxes(cache_a_idxk, 2, 3)
    cache_kro_t = jnp.swapaxes(cache_c_krope, 2, 3)

    xp = x_prompt
    xs = x_sample.reshape(1, dbsz, D_MODEL)
    outs = {k: [] for k in ('pk', 'pv', 'pik', 'pcl', 'pcr', 'pfc', 'sk', 'sv', 'sik', 'scl', 'scr', 'sfc', 'sbv')}
    yp = ys = None
    for l in range(depth):
        lw = _layer_weights(l, w_in, attn_norm, c_q_norm, c_w_uq, c_kv_norm, c_w_uk, c_w_uv, b_v_norm, b_w_s,
                            b_bias, w_br_a, w_br_b, w_br_c, w_out, ffn_norm, ffn_w_up, ffn_conv_w, ffn_conv_b,
                            ffn_w_down)
        final = l == depth - 1

        pr = _project(xp, dict(lw, gate_w=lw['b_w_s'], gate_b=lw['b_bias_t']), tabs_p, tm=TM_PROJ, decode=False)
        oa = _dsa_prompt(pr, tq=TQ_DSA, topk=topk_p, n_buckets=n_buckets)
        ol = _mla_prompt(pr, tq=TQ_MLA, n_buckets=n_buckets)
        x1 = _mixer(xp, oa, pr['ob'], ol, lw, tm=TM_MIX)
        res = _ffn(x1, lw, fin, tm=TM_FFN, final=final)
        xp = res[0]
        if final:
            yp = res[2]
        outs['pk'].append(pr['ka'].reshape(bsz, seq, A_KV_HEADS, A_HEAD_DIM))
        outs['pv'].append(pr['va'].reshape(bsz, seq, A_KV_HEADS, A_HEAD_DIM))
        outs['pik'].append(pr['ki']); outs['pcl'].append(pr['ckv']); outs['pcr'].append(pr['kr'])
        outs['pfc'].append(res[1])

        sr = _project(xs, dict(lw, gate_w=lw['b_w_00'], gate_b=lw['b_bias_0']), tabs_s, tm=dbsz, decode=True)
        seqs = lambda a, r: a.reshape(dbsz, r, a.shape[-1] // r)
        sweep_in = (_pad_rows(seqs(sr['qi'], IDX_HEADS)),
                    _pad_rows(sr['wi'][0, :, :IDX_HEADS, None]),
                    _pad_rows(seqs(sr['qlat'], C_HEADS)), _pad_rows(seqs(sr['qrope'], C_HEADS)),
                    seqs(sr['ki'], 1), seqs(sr['ckv'], 1), seqs(sr['kr'], 1))
        scores, score_new, ol_s = _sweep1(sweep_in, (cache_idx_t, cache_c_latent, cache_kro_t), page_table, l,
                                          chunk=chunk)
        sel = _select(jnp.concatenate([scores[:, 0], score_new[:, 0]], axis=1), topk=topk_s)
        oa_s = _sweep2(_pad_rows(seqs(sr['qa'], A_HEADS)), seqs(sr['ka'], 1), seqs(sr['va'], 1),
                       sel[:, None, :past_len], sel[:, None, past_len:],
                       cache_k2, cache_v2, page_table, l, chunk=chunk)
        oa_s = oa_s[:, :A_HEADS].reshape(1, dbsz, A_WIDTH).astype(bf16)
        ol_s = ol_s[:, :C_HEADS].reshape(1, dbsz, C_HEADS * KV_LORA).astype(bf16)
        x1s = _mixer(xs, oa_s, sr['ob'], ol_s, lw, tm=dbsz)
        hist = state_ffn_conv[l]
        res_s = _ffn(x1s, lw, fin, tm=dbsz, final=final,
                     hist=(hist[None, :, 0], hist[None, :, 1]))
        xs = res_s[0]
        if final:
            ys = res_s[2]
        outs['sk'].append(sr['ka'].reshape(dbsz, dseq, A_KV_HEADS, A_HEAD_DIM))
        outs['sv'].append(sr['va'].reshape(dbsz, dseq, A_KV_HEADS, A_HEAD_DIM))
        outs['sik'].append(sr['ki'].reshape(dbsz, dseq, IDX_DIM))
        outs['scl'].append(sr['ckv'].reshape(dbsz, dseq, KV_LORA))
        outs['scr'].append(sr['kr'].reshape(dbsz, dseq, C_ROPE))
        outs['sfc'].append(jnp.stack([hist[:, 1], res_s[1][0]], axis=1))
        outs['sbv'].append(sr['vn'].reshape(dbsz, dseq, B_WIDTH))

    st = {k: jnp.stack(v) for k, v in outs.items()}
    return (yp, ys.reshape(dbsz, dseq, D_MODEL), st['pk'], st['pv'], st['pik'], st['pcl'], st['pcr'], st['pfc'],
            st['sk'], st['sv'], st['sik'], st['scl'], st['scr'], st['sfc'], st['sbv'])
```

```python
import functools

import jax
import jax.numpy as jnp
from jax import lax
from jax.experimental import pallas as pl
from jax.experimental.pallas import tpu as pltpu

D_MODEL = 1024
PAGE_SIZE = 128
A_HEADS = 4
A_KV_HEADS = 2
A_HEAD_DIM = 128
A_WIDTH = A_HEADS * A_HEAD_DIM
A_KV_WIDTH = A_KV_HEADS * A_HEAD_DIM
IDX_HEADS = 8
IDX_DIM = 64
IDX_WIDTH = IDX_HEADS * IDX_DIM
TOPK_MAX = 256
CHUNK = 128
B_GROUPS = 4
B_GROUP_DIM = 128
B_WIDTH = B_GROUPS * B_GROUP_DIM
C_HEADS = 4
Q_LORA = 256
KV_LORA = 256
C_NOPE = 128
C_ROPE = 64
C_VDIM = 128
C_WIDTH = C_HEADS * C_VDIM
N_BRANCH = 3
D_FF = 2816
CONV_W = 3
ROPE_THETA = 10000.0
EPS = 1e-6

LANES = 128
SUBLANES = 8
VMEM_LIMIT = 56 * 1024 * 1024

_O_QA = 0
_O_KA = _O_QA + A_WIDTH
_O_VA = _O_KA + A_KV_WIDTH
_O_QI = _O_VA + A_KV_WIDTH
_O_KIKR = _O_QI + IDX_WIDTH
_O_WI = _O_KIKR + LANES
_O_U = _O_WI + LANES
_O_V = _O_U + B_WIDTH
_O_CQ = _O_V + B_WIDTH
_O_CKV = _O_CQ + Q_LORA
_W_ALL = _O_CKV + KV_LORA

_NT = (((1,), (1,)), ((), ()))
_INT_MIN = -2147483648
_NEG_INF_KEY = -2139095041
_ROW_PAD = 16
_SEARCH_GROUP_ROWS = 64
_MLA_HEAD_STACK = 2

f32 = jnp.float32
bf16 = jnp.bfloat16


def _rms(x, g):
    return x * lax.rsqrt(jnp.mean(x * x, axis=-1, keepdims=True) + EPS) * g


def _rope_full(x, cos, sin):
    parts = []
    for k in range(x.shape[1] // LANES):
        xs = x[:, k * LANES:(k + 1) * LANES]
        parts.append(xs * cos + pltpu.roll(xs, LANES // 2, 1) * sin)
    return parts[0] if len(parts) == 1 else jnp.concatenate(parts, axis=1)


def _rope_half(x, cos, sin_lo, sin_hi):
    parts = []
    for k in range(x.shape[1] // LANES):
        xs = x[:, k * LANES:(k + 1) * LANES]
        parts.append(xs * cos + pltpu.roll(xs, LANES - IDX_DIM // 2, 1) * sin_lo
                     + pltpu.roll(xs, IDX_DIM // 2, 1) * sin_hi)
    return parts[0] if len(parts) == 1 else jnp.concatenate(parts, axis=1)


def _order_key(score):
    bits = lax.bitcast_convert_type(score, jnp.int32)
    return bits ^ ((bits >> 31) & jnp.int32(0x7FFFFFFF))


def _count(mask):
    return jnp.sum(jnp.where(mask, 1.0, 0.0), axis=1, keepdims=True)


def _kth_largest_key(key, k):
    rows = key.shape[0]
    group = min(rows, _SEARCH_GROUP_ROWS)
    parts = [key[r:r + group] for r in range(0, rows, group)]

    def body(i, tbs):
        bit = jnp.left_shift(jnp.int32(1), 31 - i)
        out = []
        for part, tb in zip(parts, tbs):
            cand_b = tb | bit
            cnt = _count(part >= (cand_b ^ jnp.int32(_INT_MIN)))
            out.append(jnp.where(cnt >= k, cand_b, tb))
        return tuple(out)

    tbs = lax.fori_loop(0, 32, body, tuple(jnp.zeros((group, 1), jnp.int32) for _ in parts))
    tb = tbs[0] if len(tbs) == 1 else jnp.concatenate(tbs, axis=0)
    return tb ^ jnp.int32(_INT_MIN)


def _topk_membership(key, k, sel_ref):
    rows, n = key.shape
    thr = _kth_largest_key(key, k)
    gt = key > thr
    eq = key == thr
    need = k - _count(gt)
    sel_ref[...] = jnp.where(gt | eq, 1.0, 0.0)
    tie = (thr > _NEG_INF_KEY) & (_count(eq) > need)

    @pl.when(jnp.max(jnp.where(tie, 1.0, 0.0)) > 0.0)
    def _():
        r = lax.broadcasted_iota(jnp.int32, (LANES, LANES), 0)
        c = lax.broadcasted_iota(jnp.int32, (LANES, LANES), 1)
        upper = jnp.where(r < c, 1.0, 0.0).astype(bf16)
        run = jnp.zeros((rows, 1), f32)
        for j in range(n // LANES):
            sl = slice(j * LANES, (j + 1) * LANES)
            eq_j = jnp.where(eq[:, sl], 1.0, 0.0)
            before = jnp.dot(eq_j.astype(bf16), upper, preferred_element_type=f32) + run
            keep = gt[:, sl] | (eq[:, sl] & (before < need))
            sel_ref[:, sl] = jnp.where(keep, 1.0, 0.0)
            run = run + jnp.sum(eq_j, axis=1, keepdims=True)


def _const_spec(shape):
    nd = len(shape)
    return pl.BlockSpec(shape, lambda *_: (0,) * nd, pipeline_mode=pl.Buffered(1))


def _fill_in_place(operands, in_specs, partial, out_shape):
    operands.append(jnp.zeros(out_shape.shape, out_shape.dtype) if partial is None else partial)
    return dict(in_specs=in_specs + [pl.BlockSpec(memory_space=pl.ANY)], out_shape=out_shape,
                input_output_aliases={len(operands) - 1: 0})


def _params(sem):
    return pltpu.CompilerParams(dimension_semantics=sem, vmem_limit_bytes=VMEM_LIMIT)


_PROJ_OUTS = (('qa', A_WIDTH, bf16), ('ka', A_KV_WIDTH, f32), ('va', A_KV_WIDTH, f32), ('qi', IDX_WIDTH, bf16),
              ('ki', IDX_DIM, f32), ('kr', C_ROPE, f32), ('wi', LANES, f32), ('ob', B_WIDTH, bf16),
              ('ckv', KV_LORA, f32), ('qlat', C_HEADS * KV_LORA, bf16), ('qrope', C_HEADS * C_ROPE, bf16))
_PROJ_OUTS_PROMPT = (('ka_b', A_KV_WIDTH, bf16), ('va_b', A_KV_WIDTH, bf16), ('ki_b', IDX_DIM, bf16),
                     ('ckv_b', KV_LORA, bf16), ('kr_b', C_ROPE, bf16))
_PROJ_OUTS_DECODE = (('vn', B_WIDTH, f32),)


def _proj_kernel(x_ref, gain_ref, w_ref, cqn_ref, wuq_ref, wuk_ref, ckvn_ref, bvn_ref, ws_ref, bias_ref,
                 cosa_ref, sina_ref, cosi_ref, silo_ref, sihi_ref, *rest, decode):
    names = [n for n, _, _ in _PROJ_OUTS + (_PROJ_OUTS_DECODE if decode else _PROJ_OUTS_PROMPT)]
    o = dict(zip(names, rest[len(rest) - len(names):]))
    tm = x_ref.shape[0]
    h = _rms(x_ref[...], gain_ref[...]).astype(bf16)
    z_all = jnp.dot(h, w_ref[...], preferred_element_type=f32)

    def z(lo, hi):
        return z_all[:, lo:hi]

    cosa, sina = cosa_ref[...], sina_ref[...]
    cosi, silo, sihi = cosi_ref[...], silo_ref[...], sihi_ref[...]

    o['qa'][...] = _rope_full(z(_O_QA, _O_KA), cosa, sina).astype(bf16)
    ka = _rope_full(z(_O_KA, _O_VA), cosa, sina)
    va = z(_O_VA, _O_QI)
    o['ka'][...] = ka
    o['va'][...] = va
    o['qi'][...] = _rope_half(z(_O_QI, _O_KIKR), cosi, silo, sihi).astype(bf16)
    kikr = _rope_half(z(_O_KIKR, _O_WI), cosi, silo, sihi)
    o['ki'][...] = kikr[:, :IDX_DIM]
    o['kr'][...] = kikr[:, IDX_DIM:]
    o['wi'][...] = z(_O_WI, _O_U) * IDX_HEADS ** -0.5
    ckv = _rms(z(_O_CKV, _W_ALL), ckvn_ref[...])
    o['ckv'][...] = ckv
    if not decode:
        o['ka_b'][...] = ka.astype(bf16)
        o['va_b'][...] = va.astype(bf16)
        kikr_b = kikr.astype(bf16)
        o['ki_b'][...] = kikr_b[:, :IDX_DIM]
        o['kr_b'][...] = kikr_b[:, IDX_DIM:]
        o['ckv_b'][...] = ckv.astype(bf16)

    u = jax.nn.gelu(z(_O_U, _O_V))
    v = _rms(jax.nn.gelu(z(_O_V, _O_CQ)), bvn_ref[...])
    if decode:
        o['vn'][...] = v
        o['ob'][...] = (u * (ws_ref[...] * v + bias_ref[...])).astype(bf16)
    else:
        r = lax.broadcasted_iota(jnp.int32, (CHUNK, CHUNK), 0)
        c = lax.broadcasted_iota(jnp.int32, (CHUNK, CHUNK), 1)
        vb = v.astype(bf16)
        bias = bias_ref[...]
        for g in range(B_GROUPS):
            w_g = jnp.where(r >= c, ws_ref[g], 0.0).astype(bf16)
            cols = slice(g * B_GROUP_DIM, (g + 1) * B_GROUP_DIM)
            for k in range(tm // CHUNK):
                rows = slice(k * CHUNK, (k + 1) * CHUNK)
                mixed = jnp.dot(w_g, vb[rows, cols], preferred_element_type=f32) + bias[:, g:g + 1]
                o['ob'][rows, cols] = (u[rows, cols] * mixed).astype(bf16)

    cq = _rms(z(_O_CQ, _O_CKV), cqn_ref[...]).astype(bf16)
    qc = jnp.dot(cq, wuq_ref[...], preferred_element_type=f32)
    for hd in range(C_HEADS):
        q_nope = qc[:, hd * C_NOPE:(hd + 1) * C_NOPE].astype(bf16)
        o['qlat'][:, hd * KV_LORA:(hd + 1) * KV_LORA] = jnp.dot(
            q_nope, wuk_ref[hd], preferred_element_type=f32).astype(bf16)
    o['qrope'][...] = _rope_half(qc[:, C_HEADS * C_NOPE:], cosi, silo, sihi).astype(bf16)


_PROJ_STACKED = ('ka', 'va', 'ki', 'ckv', 'kr')


def _project(x, lw, tabs, *, tm, decode, layer=0, depth=1, stacked=None):
    B, L, _ = x.shape
    outs = _PROJ_OUTS + (_PROJ_OUTS_DECODE if decode else _PROJ_OUTS_PROMPT)
    names = [n for n, _, _ in outs]

    def tok(width):
        return pl.BlockSpec((None, tm, width), lambda b, i: (b, i, 0))

    def is_stacked(n):
        return not decode and n in _PROJ_STACKED

    weights = [lw['attn_norm'], lw['w_all'], lw['c_q_norm'], lw['c_w_uq'], lw['c_w_uk'], lw['c_kv_norm'],
               lw['b_v_norm'], lw['gate_w'], lw['gate_b']]
    operands = [x, *weights, *tabs]
    in_specs = ([tok(D_MODEL)] + [_const_spec(w.shape) for w in weights]
                + [pl.BlockSpec((tm, LANES), lambda b, i: (i, 0)) for _ in range(5)])
    aliases = {}
    for n, w, dt in outs:
        if is_stacked(n):
            aliases[len(operands)] = names.index(n)
            operands.append(jnp.zeros((depth, B, L, w), dt) if stacked is None else stacked[n])
            in_specs.append(pl.BlockSpec(memory_space=pl.ANY))
    res = pl.pallas_call(
        functools.partial(_proj_kernel, decode=decode),
        grid=(B, L // tm), in_specs=in_specs,
        out_specs=[pl.BlockSpec((None, None, tm, w), lambda b, i: (layer, b, i, 0)) if is_stacked(n) else tok(w)
                   for n, w, _ in outs],
        out_shape=[jax.ShapeDtypeStruct(((depth, B, L, w) if is_stacked(n) else (B, L, w)), dt)
                   for n, w, dt in outs],
        input_output_aliases=aliases,
        compiler_params=_params(("parallel", "parallel")),
        name="proj_decode" if decode else "proj_prompt",
    )(*operands)
    return dict(zip(names, res))


def _dsa_kernel(qi_ref, wi_ref, ki_ref, qa_ref, ka_ref, va_ref, *rest, topk, first_block):
    oa_ref, sel_ref = rest[-2:]
    tq, ext = sel_ref.shape
    j = first_block + pl.program_id(1)
    rep = A_HEADS // A_KV_HEADS
    ki = ki_ref[...]
    wi = wi_ref[...] * IDX_DIM ** -0.5
    score = jnp.zeros((tq, ext), f32)
    for hd in range(IDX_HEADS):
        s = lax.dot_general(qi_ref[:, hd * IDX_DIM:(hd + 1) * IDX_DIM], ki, _NT, preferred_element_type=f32)
        score = score + wi[:, hd:hd + 1] * jnp.maximum(s, 0.0)
    qpos = j * tq + lax.broadcasted_iota(jnp.int32, (tq, ext), 0)
    kpos = lax.broadcasted_iota(jnp.int32, (tq, ext), 1)
    key = _order_key(jnp.where(kpos <= qpos, score, -jnp.inf))
    _topk_membership(key, topk, sel_ref)
    sel = (sel_ref[...] > 0.5) & (key > _NEG_INF_KEY)
    sel2 = jnp.concatenate([sel] * rep, axis=0)
    for g in range(A_KV_HEADS):
        kv = slice(g * A_HEAD_DIM, (g + 1) * A_HEAD_DIM)
        q = jnp.concatenate([qa_ref[:, (g * rep + r) * A_HEAD_DIM:(g * rep + r + 1) * A_HEAD_DIM]
                             for r in range(rep)], axis=0)
        s = lax.dot_general(q, ka_ref[:, kv], _NT, preferred_element_type=f32) * A_HEAD_DIM ** -0.5
        s = jnp.where(sel2, s, -jnp.inf)
        p = jnp.exp(s - jnp.max(s, axis=1, keepdims=True))
        out = (jnp.dot(p.astype(bf16), va_ref[:, kv], preferred_element_type=f32)
               / jnp.sum(p, axis=1, keepdims=True)).astype(bf16)
        for r in range(rep):
            hd = g * rep + r
            oa_ref[:, hd * A_HEAD_DIM:(hd + 1) * A_HEAD_DIM] = out[r * tq:(r + 1) * tq]


def _dsa_prompt(pr, *, tq, topk, n_buckets):
    B, L, _ = pr['qa'].shape
    step = L // n_buckets
    per = step // tq
    out = None
    for bk in range(n_buckets):
        ext = (bk + 1) * step

        def qspec(width, bk=bk):
            return pl.BlockSpec((None, tq, width), lambda b, j: (b, bk * per + j, 0))

        def kspec(width, ext=ext):
            return pl.BlockSpec((None, ext, width), lambda b, j: (b, 0, 0))

        operands = [pr['qi'], pr['wi'], pr['ki_b'], pr['qa'], pr['ka_b'], pr['va_b']]
        in_specs = [qspec(IDX_WIDTH), qspec(LANES), kspec(IDX_DIM), qspec(A_WIDTH), kspec(A_KV_WIDTH),
                    kspec(A_KV_WIDTH)]
        out = pl.pallas_call(
            functools.partial(_dsa_kernel, topk=topk, first_block=bk * per),
            grid=(B, per),
            out_specs=qspec(A_WIDTH),
            scratch_shapes=[pltpu.VMEM((tq, ext), f32)],
            compiler_params=_params(("parallel", "parallel")),
            name=f"dsa_prompt_{ext}",
            **_fill_in_place(operands, in_specs, out, jax.ShapeDtypeStruct((B, L, A_WIDTH), bf16)),
        )(*operands)
    return out


def _mla_kernel(ql_ref, qr_ref, c_ref, kr_ref, *rest, first_block, diag):
    ol_ref = rest[-1]
    tq = ql_ref.shape[0]
    ext = c_ref.shape[0]
    j = first_block + pl.program_id(1)
    scale = (C_NOPE + C_ROPE) ** -0.5
    rows = _MLA_HEAD_STACK * tq
    qpos = j * tq + (lax.broadcasted_iota(jnp.int32, (rows, diag), 0) & (tq - 1))
    kpos = (ext - diag) + lax.broadcasted_iota(jnp.int32, (rows, diag), 1)
    causal = kpos <= qpos
    for h0 in range(0, C_HEADS, _MLA_HEAD_STACK):
        heads = range(h0, h0 + _MLA_HEAD_STACK)
        ql = jnp.concatenate([ql_ref[:, hd * KV_LORA:(hd + 1) * KV_LORA] for hd in heads], axis=0)
        qr = jnp.concatenate([qr_ref[:, hd * C_ROPE:(hd + 1) * C_ROPE] for hd in heads], axis=0)
        s = (lax.dot_general(ql, c_ref[...], _NT, preferred_element_type=f32)
             + lax.dot_general(qr, kr_ref[...], _NT, preferred_element_type=f32)) * scale
        s_diag = jnp.where(causal, s[:, ext - diag:], -jnp.inf)
        m = jnp.max(s_diag, axis=1, keepdims=True)
        if ext > diag:
            s_full = s[:, :ext - diag]
            m = jnp.maximum(m, jnp.max(s_full, axis=1, keepdims=True))
        p_diag = jnp.exp(s_diag - m)
        denom = jnp.sum(p_diag, axis=1, keepdims=True)
        out = jnp.dot(p_diag.astype(bf16), c_ref[ext - diag:, :], preferred_element_type=f32)
        if ext > diag:
            p_full = jnp.exp(s_full - m)
            denom = denom + jnp.sum(p_full, axis=1, keepdims=True)
            out = out + jnp.dot(p_full.astype(bf16), c_ref[:ext - diag, :], preferred_element_type=f32)
        out = (out / denom).astype(bf16)
        for n, hd in enumerate(heads):
            ol_ref[:, hd * KV_LORA:(hd + 1) * KV_LORA] = out[n * tq:(n + 1) * tq]


def _mla_prompt(pr, *, tq, n_buckets):
    B, L, _ = pr['qlat'].shape
    assert tq & (tq - 1) == 0
    step = L // n_buckets
    per = step // tq
    out = None
    for bk in range(n_buckets):
        ext = (bk + 1) * step

        def qspec(width, bk=bk):
            return pl.BlockSpec((None, tq, width), lambda b, j: (b, bk * per + j, 0))

        def kspec(width, ext=ext):
            return pl.BlockSpec((None, ext, width), lambda b, j: (b, 0, 0))

        operands = [pr['qlat'], pr['qrope'], pr['ckv_b'], pr['kr_b']]
        in_specs = [qspec(C_HEADS * KV_LORA), qspec(C_HEADS * C_ROPE), kspec(KV_LORA), kspec(C_ROPE)]
        out = pl.pallas_call(
            functools.partial(_mla_kernel, first_block=bk * per, diag=step),
            grid=(B, per),
            out_specs=qspec(C_HEADS * KV_LORA),
            compiler_params=_params(("parallel", "parallel")),
            name=f"mla_prompt_{ext}",
            **_fill_in_place(operands, in_specs, out, jax.ShapeDtypeStruct((B, L, C_HEADS * KV_LORA), bf16)),
        )(*operands)
    return out


def _mix_kernel(x_ref, oa_ref, ob_ref, ol_ref, gain_ref, wg_ref, wa_ref, wb_ref, wc_ref, wuv_ref, wo_ref, x1_ref):
    x = x_ref[...]
    h = _rms(x, gain_ref[...]).astype(bf16)
    oc = jnp.concatenate(
        [jnp.dot(ol_ref[:, hd * KV_LORA:(hd + 1) * KV_LORA], wuv_ref[hd], preferred_element_type=f32)
         for hd in range(C_HEADS)], axis=1).astype(bf16)
    merged = None
    for k, (o, w_ref) in enumerate(((oa_ref[...], wa_ref), (ob_ref[...], wb_ref), (oc, wc_ref))):
        gate = jax.nn.sigmoid(jnp.dot(h, wg_ref[:, k * D_MODEL:(k + 1) * D_MODEL], preferred_element_type=f32))
        term = gate * jnp.dot(o, w_ref[...], preferred_element_type=f32)
        merged = term if merged is None else merged + term
    x1_ref[...] = x + jnp.dot(merged.astype(bf16), wo_ref[...], preferred_element_type=f32)


def _mixer(x, oa, ob, ol, lw, *, tm):
    B, L, _ = x.shape

    def tok(width):
        return pl.BlockSpec((None, tm, width), lambda b, i: (b, i, 0))

    weights = [lw['attn_norm'], lw['w_gate'], lw['w_br_a'], lw['w_br_b'], lw['w_br_c'], lw['c_w_uv'], lw['w_out']]
    return pl.pallas_call(
        _mix_kernel,
        grid=(B, L // tm),
        in_specs=[tok(D_MODEL), tok(A_WIDTH), tok(B_WIDTH), tok(C_HEADS * KV_LORA)]
        + [_const_spec(w.shape) for w in weights],
        out_specs=tok(D_MODEL),
        out_shape=jax.ShapeDtypeStruct((B, L, D_MODEL), f32),
        compiler_params=_params(("parallel", "parallel")),
        name="mixer",
    )(x, oa, ob, ol, *weights)


def _ffn_kernel(*refs, decode, final):
    if decode:
        x1_ref, gain_ref, wa_ref, wg_ref, cw_ref, cb_ref, wd_ref, fin_ref, h0_ref, h1_ref = refs[:10]
        outs = refs[10:]
    else:
        x1_ref, gain_ref, wa_ref, wg_ref, cw_ref, cb_ref, wd_ref, fin_ref = refs[:8]
        outs = refs[8:]
    x2_ref, hist_ref = outs[0], outs[1]
    tm = x1_ref.shape[0]
    x1 = x1_ref[...]
    h = _rms(x1, gain_ref[...]).astype(bf16)
    a = jnp.dot(h, wa_ref[...], preferred_element_type=f32)
    g = jnp.dot(h, wg_ref[...], preferred_element_type=f32)
    cw = cw_ref[...]
    if decode:
        hist_ref[...] = g
        conv = g * cw[2:3] + cb_ref[...] + h0_ref[...] * cw[0:1] + h1_ref[...] * cw[1:2]
    else:
        gbuf = outs[-1]

        @pl.when(pl.program_id(1) == 0)
        def _():
            gbuf[0:SUBLANES, :] = jnp.zeros((SUBLANES, D_FF), f32)

        gbuf[SUBLANES:SUBLANES + tm, :] = g
        conv = g * cw[2:3] + cb_ref[...]
        for k in range(CONV_W - 1):
            off = SUBLANES - (CONV_W - 1) + k
            conv = conv + gbuf[off:off + tm, :] * cw[k:k + 1]
        hist_ref[...] = g[tm - (CONV_W - 1):, :]
        gbuf[0:SUBLANES, :] = g[tm - SUBLANES:, :]
    y = jnp.dot((jax.nn.silu(conv) * a).astype(bf16), wd_ref[...], preferred_element_type=f32)
    x2 = x1 + y
    x2_ref[...] = x2
    if final:
        outs[2][...] = _rms(x2, fin_ref[...])


def _ffn(x1, lw, final_norm, *, tm, final, hist=None):
    B, L, _ = x1.shape
    decode = hist is not None

    def tok(width):
        return pl.BlockSpec((None, tm, width), lambda b, i: (b, i, 0))

    weights = [lw['ffn_norm'], lw['ffn_w_a'], lw['ffn_w_g'], lw['ffn_conv_w'], lw['ffn_conv_b'], lw['ffn_w_down'],
               final_norm]
    in_specs = [tok(D_MODEL)] + [_const_spec(w.shape) for w in weights]
    args = [x1, *weights]
    out_specs = [tok(D_MODEL)]
    out_shape = [jax.ShapeDtypeStruct((B, L, D_MODEL), f32)]
    scratch = []
    if decode:
        in_specs += [tok(D_FF), tok(D_FF)]
        args += list(hist)
        out_specs.append(tok(D_FF))
        out_shape.append(jax.ShapeDtypeStruct((B, L, D_FF), f32))
    else:
        out_specs.append(pl.BlockSpec((None, CONV_W - 1, D_FF), lambda b, i: (b, 0, 0)))
        out_shape.append(jax.ShapeDtypeStruct((B, CONV_W - 1, D_FF), f32))
        scratch.append(pltpu.VMEM((tm + SUBLANES, D_FF), f32))
    if final:
        out_specs.append(tok(D_MODEL))
        out_shape.append(jax.ShapeDtypeStruct((B, L, D_MODEL), f32))
    return pl.pallas_call(
        functools.partial(_ffn_kernel, decode=decode, final=final),
        grid=(B, L // tm), in_specs=in_specs, out_specs=out_specs, out_shape=out_shape,
        scratch_shapes=scratch,
        compiler_params=_params(("parallel", "arbitrary")),
        name="ffn_decode" if decode else "ffn_prompt",
    )(*args)


def _select_kernel(sc_ref, sel_ref, *, topk):
    _topk_membership(_order_key(sc_ref[...]), topk, sel_ref)


def _select(scores, *, topk):
    return pl.pallas_call(
        functools.partial(_select_kernel, topk=topk),
        out_shape=jax.ShapeDtypeStruct(scores.shape, f32),
        compiler_params=pltpu.CompilerParams(vmem_limit_bytes=VMEM_LIMIT),
        name="decode_select",
    )(scores)


def _page_copies(pt_ref, seq, first_page, i, layer, srcs, bufs, sems, slot, layouts):
    page = pt_ref[seq, first_page + i]
    out = []
    for k, (src, buf, layout) in enumerate(zip(srcs, bufs, layouts)):
        rows = src.shape[2]
        if layout == 'rows':
            dst = buf.at[slot, pl.ds(pl.multiple_of(i * rows, rows), rows), :]
        else:
            dst = buf.at[slot, :, pl.ds(pl.multiple_of(i * PAGE_SIZE, PAGE_SIZE), PAGE_SIZE)]
        out.append(pltpu.make_async_copy(src.at[layer, page], dst, sems.at[k, slot]))
    return out


def _paged_chunks(pt_ref, seq, n_seq, layer, srcs, bufs, sems, layouts, chunk, n_chunks, compute):
    def each(sq, c, slot, act):
        def body(i, carry):
            for cp in _page_copies(pt_ref, sq, c * chunk, i, layer, srcs, bufs, sems, slot, layouts):
                act(cp)
            return carry
        lax.fori_loop(0, chunk, body, 0)

    def start(cp):
        cp.start()

    def wait(cp):
        cp.wait()

    first_slot = (seq * n_chunks) % 2

    @pl.when(seq == 0)
    def _():
        each(seq, 0, first_slot, start)

    for c in range(n_chunks):
        slot = (first_slot + c) % 2
        if c + 1 < n_chunks:
            each(seq, c + 1, 1 - slot, start)
        else:
            @pl.when(seq + 1 < n_seq)
            def _(slot=slot):
                each(seq + 1, 0, 1 - slot, start)
        each(seq, c, slot, wait)
        compute(c, slot)


def _sweep1_kernel(pt_ref, qi_ref, wi_ref, ql_ref, qr_ref, kin_ref, cn_ref, krn_ref, idx_hbm, lat_hbm, kro_hbm,
                   sc_ref, scn_ref, ol_ref, idx_buf, lat_buf, kro_buf, sems, *, layer, chunk, n_chunks):
    seq = pl.program_id(0)
    width = chunk * PAGE_SIZE
    scale_c = (C_NOPE + C_ROPE) ** -0.5
    qi, ql, qr = qi_ref[...], ql_ref[...], qr_ref[...]
    wi = wi_ref[...] * IDX_DIM ** -0.5
    state = [jnp.full((_ROW_PAD, 1), -jnp.inf, f32), jnp.zeros((_ROW_PAD, 1), f32),
             jnp.zeros((_ROW_PAD, KV_LORA), f32)]

    def online(sm, values):
        m_old, l_old, acc = state
        m_new = jnp.maximum(m_old, jnp.max(sm, axis=1, keepdims=True))
        alpha = jnp.exp(m_old - m_new)
        p = jnp.exp(sm - m_new)
        state[:] = [m_new, alpha * l_old + jnp.sum(p, axis=1, keepdims=True), alpha * acc + values(p)]

    def compute(c, slot):
        si = jnp.dot(qi, idx_buf[slot].astype(bf16), preferred_element_type=f32)
        sc_ref[:, c * width:(c + 1) * width] = jnp.sum(wi * jnp.maximum(si, 0.0), axis=0, keepdims=True)
        lat = lat_buf[slot].astype(bf16)
        sm = (lax.dot_general(ql, lat, _NT, preferred_element_type=f32)
              + jnp.dot(qr, kro_buf[slot].astype(bf16), preferred_element_type=f32)) * scale_c
        online(sm, lambda p: jnp.dot(p.astype(bf16), lat, preferred_element_type=f32))

    _paged_chunks(pt_ref, seq, pl.num_programs(0), layer, (idx_hbm, lat_hbm, kro_hbm),
                  (idx_buf, lat_buf, kro_buf), sems,
                  ('lanes', 'rows', 'lanes'), chunk, n_chunks, compute)

    s_new = jnp.sum(qi.astype(f32) * kin_ref[...], axis=1, keepdims=True)
    sc_new = jnp.sum(wi * jnp.maximum(s_new, 0.0), axis=0, keepdims=True)
    lane = lax.broadcasted_iota(jnp.int32, scn_ref.shape, 1)
    scn_ref[...] = jnp.where(lane == 0, sc_new, -jnp.inf)
    cn = cn_ref[...]
    sm = (jnp.sum(ql.astype(f32) * cn, axis=1, keepdims=True)
          + jnp.sum(qr.astype(f32) * krn_ref[...], axis=1, keepdims=True)) * scale_c
    online(sm, lambda p: p * cn)
    ol_ref[...] = state[2] / state[1]


def _seq_spec(a):
    nd = a.ndim - 1
    return pl.BlockSpec((None,) + a.shape[1:], lambda b, pt: (b,) + (0,) * nd)


def _sweep1(sr, caches, page_table, layer, *, chunk):
    S, n_pages = page_table.shape
    past_len = n_pages * PAGE_SIZE
    width = chunk * PAGE_SIZE
    grid_spec = pltpu.PrefetchScalarGridSpec(
        num_scalar_prefetch=1, grid=(S,),
        in_specs=[_seq_spec(a) for a in sr] + [pl.BlockSpec(memory_space=pl.ANY)] * len(caches),
        out_specs=[pl.BlockSpec((None, 1, past_len), lambda b, pt: (b, 0, 0)),
                   pl.BlockSpec((None, 1, LANES), lambda b, pt: (b, 0, 0)),
                   pl.BlockSpec((None, _ROW_PAD, KV_LORA), lambda b, pt: (b, 0, 0))],
        scratch_shapes=[pltpu.VMEM((2, IDX_DIM, width), f32), pltpu.VMEM((2, width, KV_LORA), f32),
                        pltpu.VMEM((2, C_ROPE, width), f32), pltpu.SemaphoreType.DMA((len(caches), 2))])
    return pl.pallas_call(
        functools.partial(_sweep1_kernel, layer=layer, chunk=chunk, n_chunks=n_pages // chunk),
        grid_spec=grid_spec,
        out_shape=[jax.ShapeDtypeStruct((S, 1, past_len), f32),
                   jax.ShapeDtypeStruct((S, 1, LANES), f32),
                   jax.ShapeDtypeStruct((S, _ROW_PAD, KV_LORA), f32)],
        compiler_params=_params(("arbitrary",)),
        name="decode_indexer_mla",
    )(page_table, *sr, *caches)


def _sweep2_kernel(pt_ref, qa_ref, kn_ref, vn_ref, sel_ref, seln_ref, k_hbm, v_hbm, oa_ref, k_buf, v_buf, sems,
                   *, layer, chunk, n_chunks):
    seq = pl.program_id(0)
    width = chunk * PAGE_SIZE
    rep = A_HEADS // A_KV_HEADS
    row = lax.broadcasted_iota(jnp.int32, (_ROW_PAD, 1), 0)
    first_group = row < rep
    scale = A_HEAD_DIM ** -0.5
    q = qa_ref[...]
    state = [jnp.full((_ROW_PAD, 1), -jnp.inf, f32), jnp.zeros((_ROW_PAD, 1), f32),
             jnp.zeros((_ROW_PAD, A_HEAD_DIM), f32)]

    def online(sm, keep, values):
        m_old, l_old, acc = state
        m_new = jnp.maximum(m_old, jnp.max(jnp.where(keep, sm, -jnp.inf), axis=1, keepdims=True))
        m_use = jnp.where(m_new == -jnp.inf, 0.0, m_new)
        alpha = jnp.exp(m_old - m_use)
        p = jnp.where(keep, jnp.exp(sm - m_use), 0.0)
        state[:] = [m_new, alpha * l_old + jnp.sum(p, axis=1, keepdims=True), alpha * acc + values(p)]

    def head_rows(buf, slot, g):
        return buf[slot, pl.ds(g, width, stride=A_KV_HEADS), :].astype(bf16)

    def compute(c, slot):
        s0 = lax.dot_general(q, head_rows(k_buf, slot, 0), _NT, preferred_element_type=f32)
        s1 = lax.dot_general(q, head_rows(k_buf, slot, 1), _NT, preferred_element_type=f32)

        def values(p):
            pb = p.astype(bf16)
            return jnp.where(first_group,
                             jnp.dot(pb, head_rows(v_buf, slot, 0), preferred_element_type=f32),
                             jnp.dot(pb, head_rows(v_buf, slot, 1), preferred_element_type=f32))

        online(jnp.where(first_group, s0, s1) * scale, sel_ref[:, c * width:(c + 1) * width] > 0.5, values)

    _paged_chunks(pt_ref, seq, pl.num_programs(0), layer, (k_hbm, v_hbm), (k_buf, v_buf), sems,
                  ('rows', 'rows'), chunk, n_chunks, compute)

    kn = kn_ref[...]
    vn = vn_ref[...]
    k_row = jnp.where(first_group, kn[:, :A_HEAD_DIM], kn[:, A_HEAD_DIM:])
    v_row = jnp.where(first_group, vn[:, :A_HEAD_DIM], vn[:, A_HEAD_DIM:])
    sm = jnp.sum(q.astype(f32) * k_row, axis=1, keepdims=True) * scale
    online(sm, seln_ref[:, 0:1] > 0.5, lambda p: p * v_row)
    oa_ref[...] = state[2] / state[1]


def _sweep2(qa, kn, vn, sel, sel_new, cache_k, cache_v, page_table, layer, *, chunk):
    S, n_pages = page_table.shape
    rows = chunk * cache_k.shape[2]
    grid_spec = pltpu.PrefetchScalarGridSpec(
        num_scalar_prefetch=1, grid=(S,),
        in_specs=[_seq_spec(a) for a in (qa, kn, vn, sel, sel_new)] + [pl.BlockSpec(memory_space=pl.ANY)] * 2,
        out_specs=pl.BlockSpec((None, _ROW_PAD, A_HEAD_DIM), lambda b, pt: (b, 0, 0)),
        scratch_shapes=[pltpu.VMEM((2, rows, A_HEAD_DIM), f32), pltpu.VMEM((2, rows, A_HEAD_DIM), f32),
                        pltpu.SemaphoreType.DMA((2, 2))])
    return pl.pallas_call(
        functools.partial(_sweep2_kernel, layer=layer, chunk=chunk, n_chunks=n_pages // chunk),
        grid_spec=grid_spec,
        out_shape=jax.ShapeDtypeStruct((S, _ROW_PAD, A_HEAD_DIM), f32),
        compiler_params=_params(("arbitrary",)),
        name="decode_dsa",
    )(page_table, qa, kn, vn, sel, sel_new, cache_k, cache_v)


def _rope_tables(pos):
    def cs(half):
        inv = ROPE_THETA ** (-jnp.arange(half, dtype=f32) / half)
        ang = pos.astype(f32)[:, None] * inv[None, :]
        return jnp.cos(ang), jnp.sin(ang)

    c, s = cs(A_HEAD_DIM // 2)
    cos_a = jnp.concatenate([c, c], axis=1)
    sin_a = jnp.concatenate([-s, s], axis=1)
    c, s = cs(IDX_DIM // 2)
    z = jnp.zeros_like(s)
    cos_i = jnp.concatenate([c, c, c, c], axis=1)
    sin_lo = jnp.concatenate([-s, z, -s, z], axis=1)
    sin_hi = jnp.concatenate([z, s, z, s], axis=1)
    return cos_a, sin_a, cos_i, sin_lo, sin_hi


def _layer_weights(l, w_in, attn_norm, c_q_norm, c_w_uq, c_kv_norm, c_w_uk, c_w_uv, b_v_norm, b_w_s, b_bias,
                   w_br_a, w_br_b, w_br_c, w_out, ffn_norm, ffn_w_up, ffn_conv_w, ffn_conv_b, ffn_w_down):
    w = w_in[l]
    o_ki = A_WIDTH + 2 * A_KV_WIDTH + IDX_WIDTH
    o_wi = o_ki + IDX_DIM
    o_u = o_wi + IDX_HEADS
    o_cq = o_u + 2 * B_WIDTH
    o_kr = o_cq + Q_LORA + KV_LORA
    o_gl = o_kr + C_ROPE
    w_all = jnp.concatenate(
        [w[:, :o_ki], w[:, o_ki:o_wi], w[:, o_kr:o_gl],
         jnp.pad(w[:, o_wi:o_u], ((0, 0), (0, LANES - IDX_HEADS))), w[:, o_u:o_cq], w[:, o_cq:o_kr]], axis=1)
    uq = c_w_uq[l]
    return {
        'attn_norm': attn_norm[l][None], 'w_all': w_all.astype(bf16), 'w_gate': w[:, o_gl:].astype(bf16),
        'c_q_norm': c_q_norm[l][None], 'c_kv_norm': c_kv_norm[l][None], 'b_v_norm': b_v_norm[l][None],
        'c_w_uq': jnp.concatenate([uq[:, :, :C_NOPE].reshape(Q_LORA, C_HEADS * C_NOPE),
                                   uq[:, :, C_NOPE:].reshape(Q_LORA, C_HEADS * C_ROPE)], axis=1).astype(bf16),
        'c_w_uk': jnp.transpose(c_w_uk[l], (1, 2, 0)).astype(bf16),
        'c_w_uv': jnp.transpose(c_w_uv[l], (1, 0, 2)).astype(bf16),
        'b_w_s': b_w_s[l], 'b_bias_t': b_bias[l].T,
        'b_w_00': jnp.repeat(b_w_s[l][:, 0, 0], B_GROUP_DIM)[None], 'b_bias_0': jnp.repeat(b_bias[l][:, 0], B_GROUP_DIM)[None],
        'w_br_a': w_br_a[l].astype(bf16), 'w_br_b': w_br_b[l].astype(bf16), 'w_br_c': w_br_c[l].astype(bf16),
        'w_out': w_out[l].astype(bf16), 'ffn_norm': ffn_norm[l][None],
        'ffn_w_a': ffn_w_up[l][:, :D_FF].astype(bf16), 'ffn_w_g': ffn_w_up[l][:, D_FF:].astype(bf16),
        'ffn_conv_w': ffn_conv_w[l], 'ffn_conv_b': ffn_conv_b[l][None], 'ffn_w_down': ffn_w_down[l].astype(bf16),
    }


def _pad_rows(a):
    return jnp.pad(a, ((0, 0), (0, _ROW_PAD - a.shape[1]), (0, 0)))


TM_PROJ = 512
TM_MIX = 512
TM_FFN = 256
TQ_DSA = 256
TQ_MLA = 256
CAUSAL_BUCKETS = 8
SWEEP_PAGES = 32


def kernel(x_prompt, x_sample, cache_a_k, cache_a_v, cache_a_idxk, cache_c_latent, cache_c_krope, state_ffn_conv, page_table, attn_norm, w_in, c_q_norm, c_w_uq, c_kv_norm, c_w_uk, c_w_uv, b_v_norm, b_w_s, b_bias, w_br_a, w_br_b, w_br_c, w_out, ffn_norm, ffn_w_up, ffn_conv_w, ffn_conv_b, ffn_w_down, final_norm):
    bsz, seq, _ = x_prompt.shape
    dbsz, dseq, _ = x_sample.shape
    depth = w_in.shape[0]
    assert dseq == 1 and seq % CHUNK == 0
    n_pages = page_table.shape[1]
    past_len = n_pages * PAGE_SIZE
    assert past_len % CHUNK == 0
    topk_p = min(TOPK_MAX, seq // 4)
    topk_s = min(TOPK_MAX, (past_len + dseq) // 4)
    n_buckets = min(CAUSAL_BUCKETS, seq // max(TQ_DSA, TQ_MLA, topk_p))
    chunk = min(SWEEP_PAGES, n_pages)
    assert n_pages % chunk == 0
    tabs_p = _rope_tables(jnp.arange(seq, dtype=jnp.int32))
    tabs_s = _rope_tables(jnp.full((dbsz,), past_len, dtype=jnp.int32))
    fin = final_norm[None]
    cache_k2 = cache_a_k.reshape(cache_a_k.shape[:2] + (PAGE_SIZE * A_KV_HEADS, A_HEAD_DIM))
    cache_v2 = cache_a_v.reshape(cache_a_v.shape[:2] + (PAGE_SIZE * A_KV_HEADS, A_HEAD_DIM))
    cache_idx_t = jnp.swapaxes(cache_a_idxk, 2, 3)
    cache_kro_t = jnp.swapaxes(cache_c_krope, 2, 3)

    xp = x_prompt
    xs = x_sample.reshape(1, dbsz, D_MODEL)
    outs = {k: [] for k in ('pfc', 'sk', 'sv', 'sik', 'scl', 'scr', 'sfc', 'sbv')}
    yp = ys = stacked = None
    for l in range(depth):
        lw = _layer_weights(l, w_in, attn_norm, c_q_norm, c_w_uq, c_kv_norm, c_w_uk, c_w_uv, b_v_norm, b_w_s,
                            b_bias, w_br_a, w_br_b, w_br_c, w_out, ffn_norm, ffn_w_up, ffn_conv_w, ffn_conv_b,
                            ffn_w_down)
        final = l == depth - 1

        pr = _project(xp, dict(lw, gate_w=lw['b_w_s'], gate_b=lw['b_bias_t']), tabs_p, tm=TM_PROJ, decode=False,
                      layer=l, depth=depth, stacked=stacked)
        stacked = {n: pr[n] for n in _PROJ_STACKED}
        oa = _dsa_prompt(pr, tq=TQ_DSA, topk=topk_p, n_buckets=n_buckets)
        ol = _mla_prompt(pr, tq=TQ_MLA, n_buckets=n_buckets)
        x1 = _mixer(xp, oa, pr['ob'], ol, lw, tm=TM_MIX)
        res = _ffn(x1, lw, fin, tm=TM_FFN, final=final)
        xp = res[0]
        if final:
            yp = res[2]
        outs['pfc'].append(res[1])

        sr = _project(xs, dict(lw, gate_w=lw['b_w_00'], gate_b=lw['b_bias_0']), tabs_s, tm=dbsz, decode=True)
        seqs = lambda a, r: a.reshape(dbsz, r, a.shape[-1] // r)
        sweep_in = (_pad_rows(seqs(sr['qi'], IDX_HEADS)),
                    _pad_rows(sr['wi'][0, :, :IDX_HEADS, None]),
                    _pad_rows(seqs(sr['qlat'], C_HEADS)), _pad_rows(seqs(sr['qrope'], C_HEADS)),
                    seqs(sr['ki'], 1), seqs(sr['ckv'], 1), seqs(sr['kr'], 1))
        scores, score_new, ol_s = _sweep1(sweep_in, (cache_idx_t, cache_c_latent, cache_kro_t), page_table, l,
                                          chunk=chunk)
        sel = _select(jnp.concatenate([scores[:, 0], score_new[:, 0]], axis=1), topk=topk_s)
        oa_s = _sweep2(_pad_rows(seqs(sr['qa'], A_HEADS)), seqs(sr['ka'], 1), seqs(sr['va'], 1),
                       sel[:, None, :past_len], sel[:, None, past_len:],
                       cache_k2, cache_v2, page_table, l, chunk=chunk)
        oa_s = oa_s[:, :A_HEADS].reshape(1, dbsz, A_WIDTH).astype(bf16)
        ol_s = ol_s[:, :C_HEADS].reshape(1, dbsz, C_HEADS * KV_LORA).astype(bf16)
        x1s = _mixer(xs, oa_s, sr['ob'], ol_s, lw, tm=dbsz)
        hist = state_ffn_conv[l]
        res_s = _ffn(x1s, lw, fin, tm=dbsz, final=final,
                     hist=(hist[None, :, 0], hist[None, :, 1]))
        xs = res_s[0]
        if final:
            ys = res_s[2]
        outs['sk'].append(sr['ka'].reshape(dbsz, dseq, A_KV_HEADS, A_HEAD_DIM))
        outs['sv'].append(sr['va'].reshape(dbsz, dseq, A_KV_HEADS, A_HEAD_DIM))
        outs['sik'].append(sr['ki'].reshape(dbsz, dseq, IDX_DIM))
        outs['scl'].append(sr['ckv'].reshape(dbsz, dseq, KV_LORA))
        outs['scr'].append(sr['kr'].reshape(dbsz, dseq, C_ROPE))
        outs['sfc'].append(jnp.stack([hist[:, 1], res_s[1][0]], axis=1))
        outs['sbv'].append(sr['vn'].reshape(dbsz, dseq, B_WIDTH))

    st = {k: jnp.stack(v) for k, v in outs.items()}
    kv_shape = (depth, bsz, seq, A_KV_HEADS, A_HEAD_DIM)
    return (yp, ys.reshape(dbsz, dseq, D_MODEL), stacked['ka'].reshape(kv_shape), stacked['va'].reshape(kv_shape),
            stacked['ki'], stacked['ckv'], stacked['kr'], st['pfc'],
            st['sk'], st['sv'], st['sik'], st['scl'], st['scr'], st['sfc'], st['sbv'])
```
